```python
import jax, jax.numpy as jnp
from jax import lax
import numpy as np

D_MODEL = 1024
BATCH = 4
SEQ = 4096
DEPTH = 2
DEC_BATCH = 8
DEC_SEQ = 2048
PAST_LEN = 128

ATTN_WIDTH = D_MODEL // 2
CONV_WIDTH = D_MODEL - ATTN_WIDTH
HEAD_DIM = 64
N_HEADS = ATTN_WIDTH // HEAD_DIM
N_KV_HEADS = 2
KV_GROUP = N_HEADS // N_KV_HEADS
KV_WIDTH = N_KV_HEADS * HEAD_DIM
IN_WIDTH = ATTN_WIDTH + 2 * KV_WIDTH + 2 * CONV_WIDTH
CONV_K = 31
GRID_W = 64
ROPE_THETA = 10000.0
AXIS_DIM = HEAD_DIM // 2
Q_BLOCK = 128
N_EXPERTS = 16
EXPERT_FF = 2 * D_MODEL
CAPACITY_FACTOR = 2
EPS = 1e-6

kernel_name = "hymba_attn_conformer_ec_moe_encoder"


def rms_norm(x, g):
    xf = x.astype(jnp.float32)
    y = xf * lax.rsqrt(jnp.mean(xf * xf, axis=-1, keepdims=True) + EPS)
    return (y * g.astype(jnp.float32)).astype(x.dtype)


def layer_norm(x, g, b):
    xf = x.astype(jnp.float32)
    mu = jnp.mean(xf, axis=-1, keepdims=True)
    xc = xf - mu
    y = xc * lax.rsqrt(jnp.mean(xc * xc, axis=-1, keepdims=True) + EPS)
    return (y * g.astype(jnp.float32) + b.astype(jnp.float32)).astype(x.dtype)


def axial_rope(n_tokens):
    rows = n_tokens // GRID_W
    row = jnp.repeat(jnp.arange(rows, dtype=jnp.float32), GRID_W)
    col = jnp.tile(jnp.arange(GRID_W, dtype=jnp.float32), rows)
    inv_freq = ROPE_THETA ** (-jnp.arange(0, AXIS_DIM, 2, dtype=jnp.float32) / AXIS_DIM)
    ang = jnp.concatenate([row[:, None] * inv_freq, col[:, None] * inv_freq], axis=-1)
    return jnp.cos(ang), jnp.sin(ang)


def apply_rope(x, cos, sin):
    xf = x.astype(jnp.float32).reshape(x.shape[:-1] + (HEAD_DIM // 2, 2))
    x0, x1 = xf[..., 0], xf[..., 1]
    c = cos[None, :, None, :]
    s = sin[None, :, None, :]
    out = jnp.stack([x0 * c - x1 * s, x0 * s + x1 * c], axis=-1).reshape(x.shape)
    return out.astype(x.dtype)


def blocked_attention(q, k, v):
    B, S, H, Dh = q.shape
    nb = S // Q_BLOCK
    qb = q.reshape(B, nb, Q_BLOCK, N_KV_HEADS, KV_GROUP, Dh).transpose(1, 0, 2, 3, 4, 5)
    scale = HEAD_DIM ** -0.5

    def one_block(qblk):
        s = jnp.einsum('bqkgd,bskd->bkgqs', qblk, k, preferred_element_type=jnp.float32) * scale
        p = jax.nn.softmax(s, axis=-1)
        return jnp.einsum('bkgqs,bskd->bqkgd', p.astype(v.dtype), v)

    o = lax.map(one_block, qb)
    return o.transpose(1, 0, 2, 3, 4, 5).reshape(B, S, H * Dh)


def conformer_conv(cv, cg, dw_w, dw_b, ln_g, ln_b):
    u = cv * jax.nn.sigmoid(cg)
    u = lax.conv_general_dilated(
        u, dw_w[:, None, :].astype(u.dtype), window_strides=(1,),
        padding=[(CONV_K // 2, CONV_K // 2)],
        dimension_numbers=('NWC', 'WIO', 'NWC'),
        feature_group_count=CONV_WIDTH) + dw_b.astype(u.dtype)
    u = layer_norm(u, ln_g, ln_b)
    return jax.nn.silu(u)


def expert_choice_ffn(h, w_router, w_gate, w_up, w_down):
    B, S, D = h.shape
    n = B * S
    cap = CAPACITY_FACTOR * n // N_EXPERTS
    hf = h.reshape(n, D)
    logits = jnp.einsum('nd,de->en', hf, w_router, preferred_element_type=jnp.float32)
    aff = jax.nn.softmax(logits, axis=0)
    gate, idx = lax.top_k(aff, cap)
    xg = hf[idx]
    a = jnp.einsum('ecd,edf->ecf', xg, w_gate)
    b = jnp.einsum('ecd,edf->ecf', xg, w_up)
    out = jnp.einsum('ecf,efd->ecd', jax.nn.silu(a) * b, w_down) * gate[..., None].astype(h.dtype)
    y = jnp.zeros_like(hf).at[idx.reshape(-1)].add(out.reshape(-1, D))
    return y.reshape(B, S, D)


def encoder_layer(x, cos, sin, attn_norm_g, w_in, q_norm_g, k_norm_g, conv_dw_w, conv_dw_b,
                  conv_ln_g, conv_ln_b, attn_out_g, conv_out_g, w_out, ffn_norm_g,
                  w_router, w_gate, w_up, w_down):
    B, S, _ = x.shape
    h = rms_norm(x, attn_norm_g)
    proj = h @ w_in
    q, k, v, cv, cg = jnp.split(proj, [ATTN_WIDTH, ATTN_WIDTH + KV_WIDTH, ATTN_WIDTH + 2 * KV_WIDTH,
                                       ATTN_WIDTH + 2 * KV_WIDTH + CONV_WIDTH], axis=-1)
    q = q.reshape(B, S, N_HEADS, HEAD_DIM)
    k = k.reshape(B, S, N_KV_HEADS, HEAD_DIM)
    v = v.reshape(B, S, N_KV_HEADS, HEAD_DIM)
    q = apply_rope(rms_norm(q, q_norm_g), cos, sin)
    k = apply_rope(rms_norm(k, k_norm_g), cos, sin)
    attn = blocked_attention(q, k, v)
    conv = conformer_conv(cv, cg, conv_dw_w, conv_dw_b, conv_ln_g, conv_ln_b)
    mixed = jnp.concatenate([rms_norm(attn, attn_out_g), rms_norm(conv, conv_out_g)], axis=-1)
    x = x + mixed @ w_out
    x = x + expert_choice_ffn(rms_norm(x, ffn_norm_g), w_router, w_gate, w_up, w_down)
    return x


def encoder_trunk(x, attn_norm_g, w_in, q_norm_g, k_norm_g, conv_dw_w, conv_dw_b, conv_ln_g,
                  conv_ln_b, attn_out_g, conv_out_g, w_out, ffn_norm_g, w_router, w_gate, w_up, w_down):
    cos, sin = axial_rope(x.shape[1])
    for l in range(DEPTH):
        x = encoder_layer(x, cos, sin, attn_norm_g[l], w_in[l], q_norm_g[l], k_norm_g[l],
                          conv_dw_w[l], conv_dw_b[l], conv_ln_g[l], conv_ln_b[l],
                          attn_out_g[l], conv_out_g[l], w_out[l], ffn_norm_g[l],
                          w_router[l], w_gate[l], w_up[l], w_down[l])
    return x


def setup_inputs(seed: int = 0) -> dict:
    key = jax.random.key(seed)
    ks = jax.random.split(key, 20)
    f32 = jnp.float32

    def nrm(k, shape, scale):
        return jax.random.normal(k, shape, f32) * scale

    def gain(k, shape):
        return 1.0 + 0.02 * jax.random.normal(k, shape, f32)

    return {
        "x_prompt": nrm(ks[0], (BATCH, SEQ, D_MODEL), 1.0),
        "x_sample": nrm(ks[1], (DEC_BATCH, DEC_SEQ, D_MODEL), 1.0),
        "attn_norm_g": gain(ks[2], (DEPTH, D_MODEL)),
        "w_in": nrm(ks[3], (DEPTH, D_MODEL, IN_WIDTH), D_MODEL ** -0.5),
        "q_norm_g": gain(ks[4], (DEPTH, HEAD_DIM)),
        "k_norm_g": gain(ks[5], (DEPTH, HEAD_DIM)),
        "conv_dw_w": nrm(ks[6], (DEPTH, CONV_K, CONV_WIDTH), CONV_K ** -0.5),
        "conv_dw_b": nrm(ks[7], (DEPTH, CONV_WIDTH), 0.02),
        "conv_ln_g": gain(ks[8], (DEPTH, CONV_WIDTH)),
        "conv_ln_b": nrm(ks[9], (DEPTH, CONV_WIDTH), 0.02),
        "attn_out_g": gain(ks[10], (DEPTH, ATTN_WIDTH)),
        "conv_out_g": gain(ks[11], (DEPTH, CONV_WIDTH)),
        "w_out": nrm(ks[12], (DEPTH, D_MODEL, D_MODEL), D_MODEL ** -0.5),
        "ffn_norm_g": gain(ks[13], (DEPTH, D_MODEL)),
        "w_router": nrm(ks[14], (DEPTH, D_MODEL, N_EXPERTS), D_MODEL ** -0.5),
        "w_gate": nrm(ks[15], (DEPTH, N_EXPERTS, D_MODEL, EXPERT_FF), D_MODEL ** -0.5),
        "w_up": nrm(ks[16], (DEPTH, N_EXPERTS, D_MODEL, EXPERT_FF), D_MODEL ** -0.5),
        "w_down": nrm(ks[17], (DEPTH, N_EXPERTS, EXPERT_FF, D_MODEL), EXPERT_FF ** -0.5),
    }


def reference(x_prompt, x_sample, attn_norm_g, w_in, q_norm_g, k_norm_g, conv_dw_w, conv_dw_b,
              conv_ln_g, conv_ln_b, attn_out_g, conv_out_g, w_out, ffn_norm_g, w_router,
              w_gate, w_up, w_down):
    y_prompt = encoder_trunk(x_prompt, attn_norm_g, w_in, q_norm_g, k_norm_g, conv_dw_w, conv_dw_b,
                             conv_ln_g, conv_ln_b, attn_out_g, conv_out_g, w_out, ffn_norm_g,
                             w_router, w_gate, w_up, w_down)
    y_sample = encoder_trunk(x_sample, attn_norm_g, w_in, q_norm_g, k_norm_g, conv_dw_w, conv_dw_b,
                             conv_ln_g, conv_ln_b, attn_out_g, conv_out_g, w_out, ffn_norm_g,
                             w_router, w_gate, w_up, w_down)
    return (y_prompt, y_sample)
```

```python
import functools

import numpy as np
import jax
import jax.numpy as jnp
from jax import lax
from jax.experimental import pallas as pl
from jax.experimental.pallas import tpu as pltpu

F32 = jnp.float32
BF16 = jnp.bfloat16

D_MODEL = 1024
HEAD_DIM = 64
N_HEADS = 8
N_KV_HEADS = 2
ATTN_WIDTH = N_HEADS * HEAD_DIM
KV_WIDTH = N_KV_HEADS * HEAD_DIM
KV_DUP_WIDTH = 2 * KV_WIDTH
CONV_WIDTH = D_MODEL - ATTN_WIDTH
CONV_K = 31
GRID_W = 64
ROPE_THETA = 10000.0
AXIS_DIM = HEAD_DIM // 2
N_EXPERTS = 16
EXPERT_FF = 2 * D_MODEL
CAPACITY_FACTOR = 2
EPS = 1e-6

LANES = 128
SUBLANES = 8
VMEM_LIMIT_BYTES = 56 * 1024 * 1024

TOKEN_TILE = 512
Q_TILE = 256
CONV_TILE = 128
CONV_HALO = 16
SLOT_TILE = 256
GATHER_CHUNK = 256
COMBINE_TILE = 256
FF_CHUNK = 512
FFN_ROWS = 512


def _params(*semantics):
    return pltpu.CompilerParams(dimension_semantics=semantics, vmem_limit_bytes=VMEM_LIMIT_BYTES)


def _split_bf16(a):
    hi = a.astype(BF16)
    lo = (a - hi.astype(F32)).astype(BF16)
    return hi, lo


W_Q = (0, 512)
W_QS = (512, 1024)
W_K = (1024, 1280)
W_KS = (1280, 1536)
W_V = (1536, 1792)
W_CV = (1792, 2304)
W_CG = (2304, 2816)
IN_COLS = 2816


def _in_proj_kernel(x_ref, g_ref, w_ref, bq_ref, bk_ref, gq_ref, gqs_ref, gk_ref, gks_ref,
                    cq_ref, sq_ref, ck_ref, sk_ref, q_out, k_out, v_out, u_out):
    x = x_ref[...]
    ms = jnp.mean(x * x, axis=-1, keepdims=True)
    h = (x * lax.rsqrt(ms + EPS) * g_ref[...]).astype(BF16)
    proj = jnp.dot(h, w_ref[...], preferred_element_type=F32)

    def head_rsqrt(a, ones_blockdiag):
        hi, lo = _split_bf16(a * a)
        ss = (jnp.dot(hi, ones_blockdiag, preferred_element_type=F32)
              + jnp.dot(lo, ones_blockdiag, preferred_element_type=F32))
        return lax.rsqrt(ss * (1.0 / HEAD_DIM) + EPS)

    q = proj[:, W_Q[0]:W_Q[1]]
    qs = proj[:, W_QS[0]:W_QS[1]]
    rq = head_rsqrt(q, bq_ref[...])
    q_out[...] = ((q * rq * gq_ref[...]) * cq_ref[...]
                  + (qs * rq * gqs_ref[...]) * sq_ref[...]).astype(BF16)

    k = proj[:, W_K[0]:W_K[1]]
    ks = proj[:, W_KS[0]:W_KS[1]]
    rk = head_rsqrt(k, bk_ref[...])
    k_out[...] = ((k * rk * gk_ref[...]) * ck_ref[...]
                  + (ks * rk * gks_ref[...]) * sk_ref[...]).astype(BF16)

    v_out[...] = proj[:, W_V[0]:W_V[1]].astype(BF16)
    cv = proj[:, W_CV[0]:W_CV[1]]
    cg = proj[:, W_CG[0]:W_CG[1]]
    u_out[...] = cv * jax.nn.sigmoid(cg)


def _in_proj(x2d, seq, lw, tabs):
    n = x2d.shape[0]
    tm = TOKEN_TILE
    seq_tiles = seq // tm
    row = lambda i: (i, 0)
    fixed = lambda i: (0, 0)
    pos = lambda i: (i % seq_tiles, 0)
    full = lambda a: pl.BlockSpec(a.shape, fixed)
    return pl.pallas_call(
        _in_proj_kernel,
        grid=(n // tm,),
        in_specs=[
            pl.BlockSpec((tm, D_MODEL), row),
            full(lw["attn_norm_g"]), full(lw["w_in"]), full(tabs["bq"]), full(tabs["bk"]),
            full(lw["gq"]), full(lw["gqs"]), full(lw["gk"]), full(lw["gks"]),
            pl.BlockSpec((tm, ATTN_WIDTH), pos), pl.BlockSpec((tm, ATTN_WIDTH), pos),
            pl.BlockSpec((tm, KV_DUP_WIDTH), pos), pl.BlockSpec((tm, KV_DUP_WIDTH), pos),
        ],
        out_specs=[
            pl.BlockSpec((tm, ATTN_WIDTH), row),
            pl.BlockSpec((tm, KV_DUP_WIDTH), row),
            pl.BlockSpec((tm, KV_DUP_WIDTH), row),
            pl.BlockSpec((tm, CONV_WIDTH), row),
        ],
        out_shape=[
            jax.ShapeDtypeStruct((n, ATTN_WIDTH), BF16),
            jax.ShapeDtypeStruct((n, KV_DUP_WIDTH), BF16),
            jax.ShapeDtypeStruct((n, KV_DUP_WIDTH), BF16),
            jax.ShapeDtypeStruct((n, CONV_WIDTH), F32),
        ],
        compiler_params=_params("arbitrary"),
        name="in_proj",
    )(x2d, lw["attn_norm_g"], lw["w_in"], tabs["bq"], tabs["bk"],
      lw["gq"], lw["gqs"], lw["gk"], lw["gks"],
      tabs["cq"], tabs["sq"], tabs["ck"], tabs["sk"])


def _attention_kernel(q_ref, k_ref, v_ref, o_ref):
    q = q_ref[0]
    tq = q.shape[0]
    lane = lax.broadcasted_iota(jnp.int32, q.shape, 1)
    first = lane < HEAD_DIM
    zero = jnp.zeros_like(q)
    q2 = jnp.concatenate([jnp.where(first, q, zero), jnp.where(first, zero, q)], axis=0)
    s = lax.dot_general(q2, k_ref[0], (((1,), (1,)), ((), ())), preferred_element_type=F32)
    m = jnp.max(s, axis=-1, keepdims=True)
    p = jnp.exp(s - m)
    l = jnp.sum(p, axis=-1, keepdims=True)
    o2 = jnp.dot(p.astype(BF16), v_ref[0], preferred_element_type=F32) / l
    o_ref[0] = jnp.where(first, o2[:tq], o2[tq:])


def _attention(q, k, v):
    b, s, _ = q.shape
    pairs = N_HEADS // 2
    pairs_per_kv = pairs // N_KV_HEADS
    return pl.pallas_call(
        _attention_kernel,
        grid=(b, pairs, s // Q_TILE),
        in_specs=[
            pl.BlockSpec((1, Q_TILE, LANES), lambda bi, j, i: (bi, i, j)),
            pl.BlockSpec((1, s, LANES), lambda bi, j, i: (bi, 0, j // pairs_per_kv)),
            pl.BlockSpec((1, s, LANES), lambda bi, j, i: (bi, 0, j // pairs_per_kv)),
        ],
        out_specs=pl.BlockSpec((1, Q_TILE, LANES), lambda bi, j, i: (bi, i, j)),
        out_shape=jax.ShapeDtypeStruct((b, s, ATTN_WIDTH), F32),
        compiler_params=_params("arbitrary", "arbitrary", "arbitrary"),
        name="attention",
    )(q, k, v)


def _conv_kernel(up_ref, uc_ref, un_ref, w_ref, b_ref, lng_ref, lnb_ref, og_ref, o_ref):
    i = pl.program_id(1)
    last = pl.num_programs(1) - 1
    prev = jnp.where(i > 0, up_ref[0], 0.0)
    nxt = jnp.where(i < last, un_ref[0], 0.0)
    win = jnp.concatenate([prev, uc_ref[0], nxt], axis=0)
    tr = uc_ref.shape[1]
    w = w_ref[...]
    lead = CONV_HALO - CONV_K // 2
    shifted_len = tr + 2 * CONV_HALO - SUBLANES
    acc = jnp.zeros((tr, CONV_WIDTH), F32) + b_ref[...]
    for rho in range(SUBLANES):
        taps = [k for k in range(CONV_K) if (k + lead) % SUBLANES == rho]
        if not taps:
            continue
        sh = win[rho:rho + shifted_len]
        for k in taps:
            a = (k + lead) - rho
            acc = acc + sh[a:a + tr] * w[k:k + 1, :]
    mu = jnp.mean(acc, axis=-1, keepdims=True)
    xc = acc - mu
    y = xc * lax.rsqrt(jnp.mean(xc * xc, axis=-1, keepdims=True) + EPS) * lng_ref[...] + lnb_ref[...]
    y = y * jax.nn.sigmoid(y)
    z = y * lax.rsqrt(jnp.mean(y * y, axis=-1, keepdims=True) + EPS) * og_ref[...]
    o_ref[0] = z.astype(BF16)


def _conv(u, lw):
    b, s, c = u.shape
    tr = CONV_TILE
    halo_per_tile = tr // CONV_HALO
    n_halo = s // CONV_HALO
    fixed = lambda bi, i: (0, 0)
    full = lambda a: pl.BlockSpec(a.shape, fixed)
    return pl.pallas_call(
        _conv_kernel,
        grid=(b, s // tr),
        in_specs=[
            pl.BlockSpec((1, CONV_HALO, c), lambda bi, i: (bi, jnp.maximum(i * halo_per_tile - 1, 0), 0)),
            pl.BlockSpec((1, tr, c), lambda bi, i: (bi, i, 0)),
            pl.BlockSpec((1, CONV_HALO, c),
                         lambda bi, i: (bi, jnp.minimum((i + 1) * halo_per_tile, n_halo - 1), 0)),
            full(lw["conv_dw_w"]), full(lw["conv_dw_b"]), full(lw["conv_ln_g"]),
            full(lw["conv_ln_b"]), full(lw["conv_out_g"]),
        ],
        out_specs=pl.BlockSpec((1, tr, c), lambda bi, i: (bi, i, 0)),
        out_shape=jax.ShapeDtypeStruct((b, s, c), BF16),
        compiler_params=_params("arbitrary", "arbitrary"),
        name="conv",
    )(u, u, u, lw["conv_dw_w"], lw["conv_dw_b"], lw["conv_ln_g"], lw["conv_ln_b"], lw["conv_out_g"])


def _out_proj_kernel(x_ref, a_ref, c_ref, ag_ref, wa_ref, wc_ref, fg_ref, wrh_ref, wrl_ref,
                     x1_out, hf_out, aff_out):
    a = a_ref[...]
    an = (a * lax.rsqrt(jnp.mean(a * a, axis=-1, keepdims=True) + EPS) * ag_ref[...]).astype(BF16)
    x1 = (x_ref[...]
          + jnp.dot(an, wa_ref[...], preferred_element_type=F32)
          + jnp.dot(c_ref[...], wc_ref[...], preferred_element_type=F32))
    x1_out[...] = x1
    hf = x1 * lax.rsqrt(jnp.mean(x1 * x1, axis=-1, keepdims=True) + EPS) * fg_ref[...]
    hf_hi, hf_lo = _split_bf16(hf)
    hf_out[...] = hf_hi
    nt = (((1,), (1,)), ((), ()))
    logits = (lax.dot_general(wrh_ref[...], hf_hi, nt, preferred_element_type=F32)
              + lax.dot_general(wrh_ref[...], hf_lo, nt, preferred_element_type=F32)
              + lax.dot_general(wrl_ref[...], hf_hi, nt, preferred_element_type=F32))
    mx = jnp.max(logits, axis=0, keepdims=True)
    ex = jnp.exp(logits - mx)
    aff_out[...] = ex / jnp.sum(ex, axis=0, keepdims=True)


def _out_proj(x2d, attn2d, conv2d, lw):
    n = x2d.shape[0]
    tm = TOKEN_TILE
    row = lambda i: (i, 0)
    fixed = lambda i: (0, 0)
    full = lambda a: pl.BlockSpec(a.shape, fixed)
    return pl.pallas_call(
        _out_proj_kernel,
        grid=(n // tm,),
        in_specs=[
            pl.BlockSpec((tm, D_MODEL), row),
            pl.BlockSpec((tm, ATTN_WIDTH), row),
            pl.BlockSpec((tm, CONV_WIDTH), row),
            full(lw["attn_out_g"]), full(lw["w_out_a"]), full(lw["w_out_c"]), full(lw["ffn_norm_g"]),
            full(lw["wr_hi"]), full(lw["wr_lo"]),
        ],
        out_specs=[
            pl.BlockSpec((tm, D_MODEL), row),
            pl.BlockSpec((tm, D_MODEL), row),
            pl.BlockSpec((N_EXPERTS, tm), lambda i: (0, i)),
        ],
        out_shape=[
            jax.ShapeDtypeStruct((n, D_MODEL), F32),
            jax.ShapeDtypeStruct((n, D_MODEL), BF16),
            jax.ShapeDtypeStruct((N_EXPERTS, n), F32),
        ],
        compiler_params=_params("arbitrary"),
        name="out_proj",
    )(x2d, attn2d, conv2d, lw["attn_out_g"], lw["w_out_a"], lw["w_out_c"], lw["ffn_norm_g"],
      lw["wr_hi"], lw["wr_lo"])


def _select_kernel(aff_ref, slot_out, off_out, *, capacity):
    a = aff_ref[...]
    e, nc, _ = a.shape
    keys = lax.bitcast_convert_type(a, jnp.int32)

    def count(pred):
        c = jnp.sum(pred.astype(F32), axis=1, keepdims=True)
        return jnp.sum(c, axis=2, keepdims=True)

    cap = float(capacity)

    def bit_step(i, t):
        cand = t | jnp.left_shift(jnp.int32(1), 30 - i)
        return jnp.where(count(keys >= cand) >= cap, cand, t)

    thr = lax.fori_loop(0, 31, bit_step, jnp.zeros((e, 1, 1), jnp.int32))
    above = keys > thr
    tied = keys == thr
    need = cap - count(above)

    li = lax.broadcasted_iota(jnp.int32, (LANES, LANES), 0)
    lj = lax.broadcasted_iota(jnp.int32, (LANES, LANES), 1)
    before_lane = (li < lj).astype(BF16)
    ci = lax.broadcasted_iota(jnp.int32, (nc, nc), 0)
    cj = lax.broadcasted_iota(jnp.int32, (nc, nc), 1)
    before_chunk = jnp.broadcast_to((cj < ci).astype(BF16)[None], (e, nc, nc))

    def prefix(pred):
        mb = pred.astype(BF16)
        inchunk = jnp.dot(mb.reshape(e * nc, LANES), before_lane,
                          preferred_element_type=F32).reshape(e, nc, LANES)
        part = lax.dot_general(before_chunk, mb, (((2,), (1,)), ((0,), (0,))),
                               preferred_element_type=F32)
        off = jnp.sum(part, axis=2, keepdims=True)
        return inchunk + off, off

    tie_rank, _ = prefix(tied)
    mask = above | (tied & (tie_rank < need))
    rank, off = prefix(mask)
    slot_out[...] = jnp.where(mask, rank, -1.0).astype(jnp.int32)
    off_out[...] = jnp.broadcast_to(off, off_out.shape).astype(jnp.int32)


def _select(aff3, capacity):
    e, nc, _ = aff3.shape
    return pl.pallas_call(
        functools.partial(_select_kernel, capacity=capacity),
        out_shape=[
            jax.ShapeDtypeStruct((e, nc, LANES), jnp.int32),
            jax.ShapeDtypeStruct((e, nc, LANES), jnp.int32),
        ],
        compiler_params=pltpu.CompilerParams(vmem_limit_bytes=VMEM_LIMIT_BYTES),
        name="select",
    )(aff3)


def _gather_kernel(base_ref, end_ref, slot_ref, h_ref, xg_ref):
    e = pl.program_id(0)
    c = pl.program_id(1)

    @pl.when(c == 0)
    def _():
        xg_ref[...] = jnp.zeros_like(xg_ref)

    base = base_ref[e, c]
    end = end_ref[e, c]
    j0 = base // SLOT_TILE
    srow = slot_ref[0]
    sub = lax.broadcasted_iota(jnp.int32, (SLOT_TILE, GATHER_CHUNK), 0)

    def place(j):
        onehot = (srow - j * SLOT_TILE == sub).astype(BF16)
        rows = jnp.dot(onehot, h_ref[...], preferred_element_type=F32)
        dst = pl.ds(pl.multiple_of(j * SLOT_TILE, SLOT_TILE), SLOT_TILE)
        xg_ref[0, dst, :] = xg_ref[0, dst, :] + rows.astype(BF16)

    @pl.when(end > base)
    def _():
        place(j0)

    @pl.when(end > (j0 + 1) * SLOT_TILE)
    def _():
        place(j0 + 1)


def _gather(base, end, slot_rows, hf, capacity):
    n = hf.shape[0]
    chunks = n // GATHER_CHUNK
    return pl.pallas_call(
        _gather_kernel,
        grid_spec=pltpu.PrefetchScalarGridSpec(
            num_scalar_prefetch=2,
            grid=(N_EXPERTS, chunks),
            in_specs=[
                pl.BlockSpec((1, 1, GATHER_CHUNK), lambda e, c, b_, e_: (e * chunks + c, 0, 0)),
                pl.BlockSpec((GATHER_CHUNK, D_MODEL), lambda e, c, b_, e_: (c, 0)),
            ],
            out_specs=pl.BlockSpec((1, capacity, D_MODEL), lambda e, c, b_, e_: (e, 0, 0)),
        ),
        out_shape=jax.ShapeDtypeStruct((N_EXPERTS, capacity, D_MODEL), BF16),
        compiler_params=_params("arbitrary", "arbitrary"),
        name="gather",
    )(base, end, slot_rows, hf)


def _ffn_kernel(xg_ref, wg_ref, wu_ref, wd_ref, o_ref, acc_ref):
    f = pl.program_id(1)

    @pl.when(f == 0)
    def _():
        acc_ref[...] = jnp.zeros_like(acc_ref)

    wg = wg_ref[0].astype(BF16)
    wu = wu_ref[0].astype(BF16)
    wd = wd_ref[0].astype(BF16)
    cap = xg_ref.shape[1]
    for r in range(cap // FFN_ROWS):
        rows = pl.ds(r * FFN_ROWS, FFN_ROWS)
        x = xg_ref[0, rows, :]
        a = jnp.dot(x, wg, preferred_element_type=F32)
        b = jnp.dot(x, wu, preferred_element_type=F32)
        h = (a * jax.nn.sigmoid(a) * b).astype(BF16)
        acc_ref[rows, :] += jnp.dot(h, wd, preferred_element_type=F32)

    @pl.when(f == pl.num_programs(1) - 1)
    def _():
        o_ref[0] = acc_ref[...].astype(BF16)


def _ffn(xg, w_gate, w_up, w_down):
    e, cap, d = xg.shape
    return pl.pallas_call(
        _ffn_kernel,
        grid=(e, EXPERT_FF // FF_CHUNK),
        in_specs=[
            pl.BlockSpec((1, cap, d), lambda ei, f: (ei, 0, 0)),
            pl.BlockSpec((1, d, FF_CHUNK), lambda ei, f: (ei, 0, f)),
            pl.BlockSpec((1, d, FF_CHUNK), lambda ei, f: (ei, 0, f)),
            pl.BlockSpec((1, FF_CHUNK, d), lambda ei, f: (ei, f, 0)),
        ],
        out_specs=pl.BlockSpec((1, cap, d), lambda ei, f: (ei, 0, 0)),
        out_shape=jax.ShapeDtypeStruct((e, cap, d), BF16),
        scratch_shapes=[pltpu.VMEM((cap, d), F32)],
        compiler_params=_params("arbitrary", "arbitrary"),
        name="ffn",
    )(xg, w_gate, w_up, w_down)


def _combine_kernel(lo_ref, hi_ref, x1_ref, slot_ref, aff_ref, w0_ref, w1_ref, y_ref):
    t = pl.program_id(0)
    e = pl.program_id(1)

    @pl.when(e == 0)
    def _():
        y_ref[...] = x1_ref[...]

    lo = lo_ref[t, e]
    hi = hi_ref[t, e]
    j0 = lo // SLOT_TILE
    lane = lax.broadcasted_iota(jnp.int32, slot_ref.shape, 1)
    pick = lane == e
    scol = jnp.sum(jnp.where(pick, slot_ref[...], 0), axis=1, keepdims=True)
    gate = jnp.sum(jnp.where(pick, aff_ref[...], 0.0), axis=1, keepdims=True)
    slots = lax.broadcasted_iota(jnp.int32, (COMBINE_TILE, SLOT_TILE), 1)

    def add_window(j, w_ref):
        onehot = (scol - j * SLOT_TILE == slots).astype(BF16)
        y_ref[...] += gate * jnp.dot(onehot, w_ref[0], preferred_element_type=F32)

    @pl.when(hi > lo)
    def _():
        add_window(j0, w0_ref)

    @pl.when(hi > (j0 + 1) * SLOT_TILE)
    def _():
        add_window(j0 + 1, w1_ref)


def _combine(lo, hi, x1, slot_t, aff_t, out):
    n = x1.shape[0]
    tiles_per_cap = out.shape[1] // SLOT_TILE
    tn = COMBINE_TILE

    def win0(t, e, lo_, hi_):
        return (e, lo_[t, e] // SLOT_TILE, 0)

    def win1(t, e, lo_, hi_):
        return (e, jnp.minimum(lo_[t, e] // SLOT_TILE + 1, tiles_per_cap - 1), 0)

    tok = lambda t, e, lo_, hi_: (t, 0)
    return pl.pallas_call(
        _combine_kernel,
        grid_spec=pltpu.PrefetchScalarGridSpec(
            num_scalar_prefetch=2,
            grid=(n // tn, N_EXPERTS),
            in_specs=[
                pl.BlockSpec((tn, D_MODEL), tok),
                pl.BlockSpec((tn, N_EXPERTS), tok),
                pl.BlockSpec((tn, N_EXPERTS), tok),
                pl.BlockSpec((1, SLOT_TILE, D_MODEL), win0),
                pl.BlockSpec((1, SLOT_TILE, D_MODEL), win1),
            ],
            out_specs=pl.BlockSpec((tn, D_MODEL), tok),
        ),
        out_shape=jax.ShapeDtypeStruct((n, D_MODEL), F32),
        compiler_params=_params("arbitrary", "arbitrary"),
        name="combine",
    )(lo, hi, x1, slot_t, aff_t, out, out)


def _head_perms():
    ev = np.arange(0, HEAD_DIM, 2)
    od = np.arange(1, HEAD_DIM, 2)
    return np.concatenate([ev, od]), np.concatenate([od, ev])


def _rope_tables(seq):
    rows = seq // GRID_W
    row = jnp.repeat(jnp.arange(rows, dtype=F32), GRID_W)
    col = jnp.tile(jnp.arange(GRID_W, dtype=F32), rows)
    inv_freq = ROPE_THETA ** (-jnp.arange(0, AXIS_DIM, 2, dtype=F32) / AXIS_DIM)
    ang = jnp.concatenate([row[:, None] * inv_freq, col[:, None] * inv_freq], axis=-1)
    cos, sin = jnp.cos(ang), jnp.sin(ang)
    c_head = jnp.concatenate([cos, cos], axis=-1)
    s_head = jnp.concatenate([-sin, sin], axis=-1)
    scale = HEAD_DIM ** -0.5
    ones_block = jnp.ones((HEAD_DIM, HEAD_DIM), F32)
    return {
        "cq": jnp.tile(c_head, (1, N_HEADS)) * scale,
        "sq": jnp.tile(s_head, (1, N_HEADS)) * scale,
        "ck": jnp.tile(c_head, (1, 2 * N_KV_HEADS)),
        "sk": jnp.tile(s_head, (1, 2 * N_KV_HEADS)),
        "bq": jnp.kron(jnp.eye(N_HEADS, dtype=F32), ones_block).astype(BF16),
        "bk": jnp.kron(jnp.eye(2 * N_KV_HEADS, dtype=F32), ones_block).astype(BF16),
    }


def _layer_weights(l, p):
    perm, perm_sw = _head_perms()
    w_in = p["w_in"][l]
    o_k = ATTN_WIDTH
    o_v = o_k + KV_WIDTH
    o_cv = o_v + KV_WIDTH
    o_cg = o_cv + CONV_WIDTH
    q_cols = np.concatenate([h * HEAD_DIM + perm for h in range(N_HEADS)])
    qs_cols = np.concatenate([h * HEAD_DIM + perm_sw for h in range(N_HEADS)])
    kv_heads = [g for g in range(N_KV_HEADS) for _ in range(2)]
    k_cols = np.concatenate([o_k + g * HEAD_DIM + perm for g in kv_heads])
    ks_cols = np.concatenate([o_k + g * HEAD_DIM + perm_sw for g in kv_heads])
    v_cols = np.concatenate([o_v + g * HEAD_DIM + np.arange(HEAD_DIM) for g in kv_heads])
    cols = np.concatenate([q_cols, qs_cols, k_cols, ks_cols, v_cols,
                           np.arange(o_cv, o_cg), np.arange(o_cg, o_cg + CONV_WIDTH)])
    assert cols.shape[0] == IN_COLS
    row2 = lambda a: a.reshape(1, -1)
    w_router_t = p["w_router"][l].T
    wr_hi, wr_lo = _split_bf16(w_router_t)
    w_out = p["w_out"][l].astype(BF16)
    return {
        "attn_norm_g": row2(p["attn_norm_g"][l]),
        "w_in": w_in[:, cols].astype(BF16),
        "gq": row2(jnp.tile(p["q_norm_g"][l][perm], N_HEADS)),
        "gqs": row2(jnp.tile(p["q_norm_g"][l][perm_sw], N_HEADS)),
        "gk": row2(jnp.tile(p["k_norm_g"][l][perm], 2 * N_KV_HEADS)),
        "gks": row2(jnp.tile(p["k_norm_g"][l][perm_sw], 2 * N_KV_HEADS)),
        "conv_dw_w": p["conv_dw_w"][l],
        "conv_dw_b": row2(p["conv_dw_b"][l]),
        "conv_ln_g": row2(p["conv_ln_g"][l]),
        "conv_ln_b": row2(p["conv_ln_b"][l]),
        "conv_out_g": row2(p["conv_out_g"][l]),
        "attn_out_g": row2(p["attn_out_g"][l]),
        "w_out_a": w_out[:ATTN_WIDTH],
        "w_out_c": w_out[ATTN_WIDTH:],
        "ffn_norm_g": row2(p["ffn_norm_g"][l]),
        "wr_hi": wr_hi,
        "wr_lo": wr_lo,
        "w_gate": p["w_gate"][l],
        "w_up": p["w_up"][l],
        "w_down": p["w_down"][l],
    }


def _moe(x1, hf, aff, lw):
    n = x1.shape[0]
    capacity = CAPACITY_FACTOR * n // N_EXPERTS
    nc = n // LANES
    slot3, off3 = _select(aff.reshape(N_EXPERTS, nc, LANES), capacity)
    bounds = jnp.concatenate([off3[:, :, 0], jnp.full((N_EXPERTS, 1), capacity, jnp.int32)], axis=1)
    g_step = GATHER_CHUNK // LANES
    base = bounds[:, 0:nc:g_step]
    end = bounds[:, g_step::g_step]
    slot2 = slot3.reshape(N_EXPERTS, n)
    xg = _gather(base, end, slot2.reshape(N_EXPERTS * (n // GATHER_CHUNK), 1, GATHER_CHUNK), hf, capacity)
    out = _ffn(xg, lw["w_gate"], lw["w_up"], lw["w_down"])
    c_step = COMBINE_TILE // LANES
    lo = bounds[:, 0:nc:c_step].T
    hi = bounds[:, c_step::c_step].T
    return _combine(lo, hi, x1, slot2.T, aff.T, out)


def _layer(x2d, batch, seq, lw, tabs):
    q, k, v, u = _in_proj(x2d, seq, lw, tabs)
    attn = _attention(q.reshape(batch, seq, ATTN_WIDTH), k.reshape(batch, seq, KV_DUP_WIDTH),
                      v.reshape(batch, seq, KV_DUP_WIDTH))
    conv = _conv(u.reshape(batch, seq, CONV_WIDTH), lw)
    x1, hf, aff = _out_proj(x2d, attn.reshape(-1, ATTN_WIDTH), conv.reshape(-1, CONV_WIDTH), lw)
    return _moe(x1, hf, aff, lw)


def _trunk(x, layers):
    batch, seq, d = x.shape
    tabs = _rope_tables(seq)
    x2d = x.reshape(batch * seq, d)
    for lw in layers:
        x2d = _layer(x2d, batch, seq, lw, tabs)
    return x2d.reshape(batch, seq, d)


def kernel(x_prompt, x_sample, attn_norm_g, w_in, q_norm_g, k_norm_g, conv_dw_w, conv_dw_b, conv_ln_g, conv_ln_b, attn_out_g, conv_out_g, w_out, ffn_norm_g, w_router, w_gate, w_up, w_down):
    p = dict(attn_norm_g=attn_norm_g, w_in=w_in, q_norm_g=q_norm_g, k_norm_g=k_norm_g,
             conv_dw_w=conv_dw_w, conv_dw_b=conv_dw_b, conv_ln_g=conv_ln_g, conv_ln_b=conv_ln_b,
             attn_out_g=attn_out_g, conv_out_g=conv_out_g, w_out=w_out, ffn_norm_g=ffn_norm_g,
             w_router=w_router, w_gate=w_gate, w_up=w_up, w_down=w_down)
    layers = [_layer_weights(l, p) for l in range(w_in.shape[0])]
    return (_trunk(x_prompt, layers), _trunk(x_sample, layers))
```

```python
import functools

import numpy as np
import jax
import jax.numpy as jnp
from jax import lax
from jax.experimental import pallas as pl
from jax.experimental.pallas import tpu as pltpu

F32 = jnp.float32
BF16 = jnp.bfloat16

D_MODEL = 1024
HEAD_DIM = 64
N_HEADS = 8
N_KV_HEADS = 2
ATTN_WIDTH = N_HEADS * HEAD_DIM
KV_WIDTH = N_KV_HEADS * HEAD_DIM
KV_DUP_WIDTH = 2 * KV_WIDTH
CONV_WIDTH = D_MODEL - ATTN_WIDTH
CONV_K = 31
GRID_W = 64
ROPE_THETA = 10000.0
AXIS_DIM = HEAD_DIM // 2
N_EXPERTS = 16
EXPERT_FF = 2 * D_MODEL
CAPACITY_FACTOR = 2
EPS = 1e-6

LANES = 128
SUBLANES = 8
VMEM_LIMIT_BYTES = 56 * 1024 * 1024

TOKEN_TILE = 512
Q_TILE = 256
CONV_TILE = 128
CONV_HALO = 16
SLOT_TILE = 256
GATHER_CHUNK = 256
COMBINE_TILE = 256
FF_CHUNK = 512
FFN_ROWS = 512
MOE_TILE = 256
BF16_ROWS = 16
GATHER_WINDOW = 80
COMBINE_WINDOW = 128
CAP_PAD = 128


def _params(*semantics):
    return pltpu.CompilerParams(dimension_semantics=semantics, vmem_limit_bytes=VMEM_LIMIT_BYTES)


def _split_bf16(a):
    hi = a.astype(BF16)
    lo = (a - hi.astype(F32)).astype(BF16)
    return hi, lo


W_Q = (0, 512)
W_QS = (512, 1024)
W_K = (1024, 1280)
W_KS = (1280, 1536)
W_V = (1536, 1792)
W_CV = (1792, 2304)
W_CG = (2304, 2816)
IN_COLS = 2816


def _in_proj_kernel(x_ref, g_ref, w_ref, bq_ref, bk_ref, gq_ref, gqs_ref, gk_ref, gks_ref,
                    cq_ref, sq_ref, ck_ref, sk_ref, q_out, k_out, v_out, u_out):
    x = x_ref[...]
    ms = jnp.mean(x * x, axis=-1, keepdims=True)
    h = (x * lax.rsqrt(ms + EPS) * g_ref[...]).astype(BF16)
    proj = jnp.dot(h, w_ref[...], preferred_element_type=F32)

    def head_rsqrt(a, ones_blockdiag):
        hi, lo = _split_bf16(a * a)
        ss = (jnp.dot(hi, ones_blockdiag, preferred_element_type=F32)
              + jnp.dot(lo, ones_blockdiag, preferred_element_type=F32))
        return lax.rsqrt(ss * (1.0 / HEAD_DIM) + EPS)

    q = proj[:, W_Q[0]:W_Q[1]]
    qs = proj[:, W_QS[0]:W_QS[1]]
    rq = head_rsqrt(q, bq_ref[...])
    q_out[...] = ((q * rq * gq_ref[...]) * cq_ref[...]
                  + (qs * rq * gqs_ref[...]) * sq_ref[...]).astype(BF16)

    k = proj[:, W_K[0]:W_K[1]]
    ks = proj[:, W_KS[0]:W_KS[1]]
    rk = head_rsqrt(k, bk_ref[...])
    k_out[...] = ((k * rk * gk_ref[...]) * ck_ref[...]
                  + (ks * rk * gks_ref[...]) * sk_ref[...]).astype(BF16)

    v_out[...] = proj[:, W_V[0]:W_V[1]].astype(BF16)
    cv = proj[:, W_CV[0]:W_CV[1]]
    cg = proj[:, W_CG[0]:W_CG[1]]
    u_out[...] = cv * jax.nn.sigmoid(cg)


def _in_proj(x2d, seq, lw, tabs):
    n = x2d.shape[0]
    tm = TOKEN_TILE
    seq_tiles = seq // tm
    row = lambda i: (i, 0)
    fixed = lambda i: (0, 0)
    pos = lambda i: (i % seq_tiles, 0)
    full = lambda a: pl.BlockSpec(a.shape, fixed)
    return pl.pallas_call(
        _in_proj_kernel,
        grid=(n // tm,),
        in_specs=[
            pl.BlockSpec((tm, D_MODEL), row),
            full(lw["attn_norm_g"]), full(lw["w_in"]), full(tabs["bq"]), full(tabs["bk"]),
            full(lw["gq"]), full(lw["gqs"]), full(lw["gk"]), full(lw["gks"]),
            pl.BlockSpec((tm, ATTN_WIDTH), pos), pl.BlockSpec((tm, ATTN_WIDTH), pos),
            pl.BlockSpec((tm, KV_DUP_WIDTH), pos), pl.BlockSpec((tm, KV_DUP_WIDTH), pos),
        ],
        out_specs=[
            pl.BlockSpec((tm, ATTN_WIDTH), row),
            pl.BlockSpec((tm, KV_DUP_WIDTH), row),
            pl.BlockSpec((tm, KV_DUP_WIDTH), row),
            pl.BlockSpec((tm, CONV_WIDTH), row),
        ],
        out_shape=[
            jax.ShapeDtypeStruct((n, ATTN_WIDTH), BF16),
            jax.ShapeDtypeStruct((n, KV_DUP_WIDTH), BF16),
            jax.ShapeDtypeStruct((n, KV_DUP_WIDTH), BF16),
            jax.ShapeDtypeStruct((n, CONV_WIDTH), F32),
        ],
        compiler_params=_params("arbitrary"),
        name="in_proj",
    )(x2d, lw["attn_norm_g"], lw["w_in"], tabs["bq"], tabs["bk"],
      lw["gq"], lw["gqs"], lw["gk"], lw["gks"],
      tabs["cq"], tabs["sq"], tabs["ck"], tabs["sk"])


def _attention_kernel(q_ref, k_ref, v_ref, o_ref):
    q = q_ref[0]
    tq = q.shape[0]
    lane = lax.broadcasted_iota(jnp.int32, q.shape, 1)
    first = lane < HEAD_DIM
    zero = jnp.zeros_like(q)
    q2 = jnp.concatenate([jnp.where(first, q, zero), jnp.where(first, zero, q)], axis=0)
    s = lax.dot_general(q2, k_ref[0], (((1,), (1,)), ((), ())), preferred_element_type=F32)
    m = jnp.max(s, axis=-1, keepdims=True)
    p = jnp.exp(s - m)
    l = jnp.sum(p, axis=-1, keepdims=True)
    o2 = jnp.dot(p.astype(BF16), v_ref[0], preferred_element_type=F32) / l
    o_ref[0] = jnp.where(first, o2[:tq], o2[tq:])


def _attention(q, k, v):
    b, s, _ = q.shape
    pairs = N_HEADS // 2
    pairs_per_kv = pairs // N_KV_HEADS
    return pl.pallas_call(
        _attention_kernel,
        grid=(b, pairs, s // Q_TILE),
        in_specs=[
            pl.BlockSpec((1, Q_TILE, LANES), lambda bi, j, i: (bi, i, j)),
            pl.BlockSpec((1, s, LANES), lambda bi, j, i: (bi, 0, j // pairs_per_kv)),
            pl.BlockSpec((1, s, LANES), lambda bi, j, i: (bi, 0, j // pairs_per_kv)),
        ],
        out_specs=pl.BlockSpec((1, Q_TILE, LANES), lambda bi, j, i: (bi, i, j)),
        out_shape=jax.ShapeDtypeStruct((b, s, ATTN_WIDTH), F32),
        compiler_params=_params("arbitrary", "arbitrary", "arbitrary"),
        name="attention",
    )(q, k, v)


def _conv_kernel(up_ref, uc_ref, un_ref, w_ref, b_ref, lng_ref, lnb_ref, og_ref, o_ref):
    i = pl.program_id(1)
    last = pl.num_programs(1) - 1
    prev = jnp.where(i > 0, up_ref[0], 0.0)
    nxt = jnp.where(i < last, un_ref[0], 0.0)
    win = jnp.concatenate([prev, uc_ref[0], nxt], axis=0)
    tr = uc_ref.shape[1]
    w = w_ref[...]
    lead = CONV_HALO - CONV_K // 2
    shifted_len = tr + 2 * CONV_HALO - SUBLANES
    acc = jnp.zeros((tr, CONV_WIDTH), F32) + b_ref[...]
    for rho in range(SUBLANES):
        taps = [k for k in range(CONV_K) if (k + lead) % SUBLANES == rho]
        if not taps:
            continue
        sh = win[rho:rho + shifted_len]
        for k in taps:
            a = (k + lead) - rho
            acc = acc + sh[a:a + tr] * w[k:k + 1, :]
    mu = jnp.mean(acc, axis=-1, keepdims=True)
    xc = acc - mu
    y = xc * lax.rsqrt(jnp.mean(xc * xc, axis=-1, keepdims=True) + EPS) * lng_ref[...] + lnb_ref[...]
    y = y * jax.nn.sigmoid(y)
    z = y * lax.rsqrt(jnp.mean(y * y, axis=-1, keepdims=True) + EPS) * og_ref[...]
    o_ref[0] = z.astype(BF16)


def _conv(u, lw):
    b, s, c = u.shape
    tr = CONV_TILE
    halo_per_tile = tr // CONV_HALO
    n_halo = s // CONV_HALO
    fixed = lambda bi, i: (0, 0)
    full = lambda a: pl.BlockSpec(a.shape, fixed)
    return pl.pallas_call(
        _conv_kernel,
        grid=(b, s // tr),
        in_specs=[
            pl.BlockSpec((1, CONV_HALO, c), lambda bi, i: (bi, jnp.maximum(i * halo_per_tile - 1, 0), 0)),
            pl.BlockSpec((1, tr, c), lambda bi, i: (bi, i, 0)),
            pl.BlockSpec((1, CONV_HALO, c),
                         lambda bi, i: (bi, jnp.minimum((i + 1) * halo_per_tile, n_halo - 1), 0)),
            full(lw["conv_dw_w"]), full(lw["conv_dw_b"]), full(lw["conv_ln_g"]),
            full(lw["conv_ln_b"]), full(lw["conv_out_g"]),
        ],
        out_specs=pl.BlockSpec((1, tr, c), lambda bi, i: (bi, i, 0)),
        out_shape=jax.ShapeDtypeStruct((b, s, c), BF16),
        compiler_params=_params("arbitrary", "arbitrary"),
        name="conv",
    )(u, u, u, lw["conv_dw_w"], lw["conv_dw_b"], lw["conv_ln_g"], lw["conv_ln_b"], lw["conv_out_g"])


def _out_proj_kernel(x_ref, a_ref, c_ref, ag_ref, wa_ref, wc_ref, fg_ref, wrh_ref, wrl_ref,
                     x1_out, hf_out, aff_out):
    a = a_ref[...]
    an = (a * lax.rsqrt(jnp.mean(a * a, axis=-1, keepdims=True) + EPS) * ag_ref[...]).astype(BF16)
    x1 = (x_ref[...]
          + jnp.dot(an, wa_ref[...], preferred_element_type=F32)
          + jnp.dot(c_ref[...], wc_ref[...], preferred_element_type=F32))
    x1_out[...] = x1
    hf = x1 * lax.rsqrt(jnp.mean(x1 * x1, axis=-1, keepdims=True) + EPS) * fg_ref[...]
    hf_hi, hf_lo = _split_bf16(hf)
    hf_out[...] = hf_hi
    nt = (((1,), (1,)), ((), ()))
    logits = (lax.dot_general(wrh_ref[...], hf_hi, nt, preferred_element_type=F32)
              + lax.dot_general(wrh_ref[...], hf_lo, nt, preferred_element_type=F32)
              + lax.dot_general(wrl_ref[...], hf_hi, nt, preferred_element_type=F32))
    mx = jnp.max(logits, axis=0, keepdims=True)
    ex = jnp.exp(logits - mx)
    aff_out[...] = ex / jnp.sum(ex, axis=0, keepdims=True)


def _out_proj(x2d, attn2d, conv2d, lw):
    n = x2d.shape[0]
    tm = TOKEN_TILE
    row = lambda i: (i, 0)
    fixed = lambda i: (0, 0)
    full = lambda a: pl.BlockSpec(a.shape, fixed)
    return pl.pallas_call(
        _out_proj_kernel,
        grid=(n // tm,),
        in_specs=[
            pl.BlockSpec((tm, D_MODEL), row),
            pl.BlockSpec((tm, ATTN_WIDTH), row),
            pl.BlockSpec((tm, CONV_WIDTH), row),
            full(lw["attn_out_g"]), full(lw["w_out_a"]), full(lw["w_out_c"]), full(lw["ffn_norm_g"]),
            full(lw["wr_hi"]), full(lw["wr_lo"]),
        ],
        out_specs=[
            pl.BlockSpec((tm, D_MODEL), row),
            pl.BlockSpec((tm, D_MODEL), row),
            pl.BlockSpec((N_EXPERTS, tm), lambda i: (0, i)),
        ],
        out_shape=[
            jax.ShapeDtypeStruct((n, D_MODEL), F32),
            jax.ShapeDtypeStruct((n, D_MODEL), BF16),
            jax.ShapeDtypeStruct((N_EXPERTS, n), F32),
        ],
        compiler_params=_params("arbitrary"),
        name="out_proj",
    )(x2d, attn2d, conv2d, lw["attn_out_g"], lw["w_out_a"], lw["w_out_c"], lw["ffn_norm_g"],
      lw["wr_hi"], lw["wr_lo"])


def _select_kernel(aff_ref, slot_out, off_out, *, capacity):
    a = aff_ref[...]
    e, nc, _ = a.shape
    keys = lax.bitcast_convert_type(a, jnp.int32)

    def count(pred):
        c = jnp.sum(pred.astype(F32), axis=1, keepdims=True)
        return jnp.sum(c, axis=2, keepdims=True)

    cap = float(capacity)

    def bit_step(i, t):
        cand = t | jnp.left_shift(jnp.int32(1), 30 - i)
        return jnp.where(count(keys >= cand) >= cap, cand, t)

    thr = lax.fori_loop(0, 31, bit_step, jnp.zeros((e, 1, 1), jnp.int32))
    above = keys > thr
    tied = keys == thr
    need = cap - count(above)

    li = lax.broadcasted_iota(jnp.int32, (LANES, LANES), 0)
    lj = lax.broadcasted_iota(jnp.int32, (LANES, LANES), 1)
    before_lane = (li < lj).astype(BF16)
    ci = lax.broadcasted_iota(jnp.int32, (nc, nc), 0)
    cj = lax.broadcasted_iota(jnp.int32, (nc, nc), 1)
    before_chunk = jnp.broadcast_to((cj < ci).astype(BF16)[None], (e, nc, nc))

    def prefix(pred):
        mb = pred.astype(BF16)
        inchunk = jnp.dot(mb.reshape(e * nc, LANES), before_lane,
                          preferred_element_type=F32).reshape(e, nc, LANES)
        part = lax.dot_general(before_chunk, mb, (((2,), (1,)), ((0,), (0,))),
                               preferred_element_type=F32)
        off = jnp.sum(part, axis=2, keepdims=True)
        return inchunk + off, off

    tie_rank, _ = prefix(tied)
    mask = above | (tied & (tie_rank < need))
    rank, off = prefix(mask)
    slot_out[...] = jnp.where(mask, rank, -1.0).astype(jnp.int32)
    off_out[...] = jnp.broadcast_to(off, off_out.shape).astype(jnp.int32)


def _select(aff3, capacity):
    e, nc, _ = aff3.shape
    return pl.pallas_call(
        functools.partial(_select_kernel, capacity=capacity),
        out_shape=[
            jax.ShapeDtypeStruct((e, nc, LANES), jnp.int32),
            jax.ShapeDtypeStruct((e, nc, LANES), jnp.int32),
        ],
        compiler_params=pltpu.CompilerParams(vmem_limit_bytes=VMEM_LIMIT_BYTES),
        name="select",
    )(aff3)


def _gather_kernel(base_ref, end_ref, slot_ref, h_ref, xg_ref):
    e = pl.program_id(0)
    c = pl.program_id(1)

    @pl.when(c == 0)
    def _():
        xg_ref[...] = jnp.zeros_like(xg_ref)

    base = base_ref[e, c]
    end = end_ref[e, c]
    j0 = base // SLOT_TILE
    srow = slot_ref[0]
    sub = lax.broadcasted_iota(jnp.int32, (SLOT_TILE, GATHER_CHUNK), 0)

    def place(j):
        onehot = (srow - j * SLOT_TILE == sub).astype(BF16)
        rows = jnp.dot(onehot, h_ref[...], preferred_element_type=F32)
        dst = pl.ds(pl.multiple_of(j * SLOT_TILE, SLOT_TILE), SLOT_TILE)
        xg_ref[0, dst, :] = xg_ref[0, dst, :] + rows.astype(BF16)

    @pl.when(end > base)
    def _():
        place(j0)

    @pl.when(end > (j0 + 1) * SLOT_TILE)
    def _():
        place(j0 + 1)


def _gather(base, end, slot_rows, hf, capacity):
    n = hf.shape[0]
    chunks = n // GATHER_CHUNK
    return pl.pallas_call(
        _gather_kernel,
        grid_spec=pltpu.PrefetchScalarGridSpec(
            num_scalar_prefetch=2,
            grid=(N_EXPERTS, chunks),
            in_specs=[
                pl.BlockSpec((1, 1, GATHER_CHUNK), lambda e, c, b_, e_: (e * chunks + c, 0, 0)),
                pl.BlockSpec((GATHER_CHUNK, D_MODEL), lambda e, c, b_, e_: (c, 0)),
            ],
            out_specs=pl.BlockSpec((1, capacity, D_MODEL), lambda e, c, b_, e_: (e, 0, 0)),
        ),
        out_shape=jax.ShapeDtypeStruct((N_EXPERTS, capacity, D_MODEL), BF16),
        compiler_params=_params("arbitrary", "arbitrary"),
        name="gather",
    )(base, end, slot_rows, hf)


def _ffn_kernel(xg_ref, wg_ref, wu_ref, wd_ref, o_ref, acc_ref):
    f = pl.program_id(1)

    @pl.when(f == 0)
    def _():
        acc_ref[...] = jnp.zeros_like(acc_ref)

    wg = wg_ref[0].astype(BF16)
    wu = wu_ref[0].astype(BF16)
    wd = wd_ref[0].astype(BF16)
    cap = acc_ref.shape[0]
    block = min(FFN_ROWS, cap)
    for r in range(cap // block):
        rows = pl.ds(r * block, block)
        x = xg_ref[0, rows, :]
        a = jnp.dot(x, wg, preferred_element_type=F32)
        b = jnp.dot(x, wu, preferred_element_type=F32)
        h = (a * jax.nn.sigmoid(a) * b).astype(BF16)
        acc_ref[rows, :] += jnp.dot(h, wd, preferred_element_type=F32)

    @pl.when(f == pl.num_programs(1) - 1)
    def _():
        o_ref[0, pl.ds(0, cap), :] = acc_ref[...].astype(BF16)
        o_ref[0, pl.ds(cap, CAP_PAD), :] = jnp.zeros((CAP_PAD, o_ref.shape[2]), BF16)


def _ffn(xg, w_gate, w_up, w_down, capacity):
    e, _, d = xg.shape
    return pl.pallas_call(
        _ffn_kernel,
        grid=(e, EXPERT_FF // FF_CHUNK),
        in_specs=[
            pl.BlockSpec((1, capacity, d), lambda ei, f: (ei, 0, 0)),
            pl.BlockSpec((1, d, FF_CHUNK), lambda ei, f: (ei, 0, f)),
            pl.BlockSpec((1, d, FF_CHUNK), lambda ei, f: (ei, 0, f)),
            pl.BlockSpec((1, FF_CHUNK, d), lambda ei, f: (ei, f, 0)),
        ],
        out_specs=pl.BlockSpec((1, capacity + CAP_PAD, d), lambda ei, f: (ei, 0, 0)),
        out_shape=jax.ShapeDtypeStruct((e, capacity + CAP_PAD, d), BF16),
        scratch_shapes=[pltpu.VMEM((capacity, d), F32)],
        compiler_params=_params("arbitrary", "arbitrary"),
        name="ffn",
    )(xg, w_gate, w_up, w_down)


def _combine_kernel(lo_ref, hi_ref, x1_ref, slot_ref, aff_ref, w0_ref, w1_ref, y_ref):
    t = pl.program_id(0)
    e = pl.program_id(1)

    @pl.when(e == 0)
    def _():
        y_ref[...] = x1_ref[...]

    lo = lo_ref[t, e]
    hi = hi_ref[t, e]
    j0 = lo // SLOT_TILE
    lane = lax.broadcasted_iota(jnp.int32, slot_ref.shape, 1)
    pick = lane == e
    scol = jnp.sum(jnp.where(pick, slot_ref[...], 0), axis=1, keepdims=True)
    gate = jnp.sum(jnp.where(pick, aff_ref[...], 0.0), axis=1, keepdims=True)
    slots = lax.broadcasted_iota(jnp.int32, (COMBINE_TILE, SLOT_TILE), 1)

    def add_window(j, w_ref):
        onehot = (scol - j * SLOT_TILE == slots).astype(BF16)
        y_ref[...] += gate * jnp.dot(onehot, w_ref[0], preferred_element_type=F32)

    @pl.when(hi > lo)
    def _():
        add_window(j0, w0_ref)

    @pl.when(hi > (j0 + 1) * SLOT_TILE)
    def _():
        add_window(j0 + 1, w1_ref)


def _combine(lo, hi, x1, slot_t, aff_t, out):
    n = x1.shape[0]
    tiles_per_cap = out.shape[1] // SLOT_TILE
    tn = COMBINE_TILE

    def win0(t, e, lo_, hi_):
        return (e, lo_[t, e] // SLOT_TILE, 0)

    def win1(t, e, lo_, hi_):
        return (e, jnp.minimum(lo_[t, e] // SLOT_TILE + 1, tiles_per_cap - 1), 0)

    tok = lambda t, e, lo_, hi_: (t, 0)
    return pl.pallas_call(
        _combine_kernel,
        grid_spec=pltpu.PrefetchScalarGridSpec(
            num_scalar_prefetch=2,
            grid=(n // tn, N_EXPERTS),
            in_specs=[
                pl.BlockSpec((tn, D_MODEL), tok),
                pl.BlockSpec((tn, N_EXPERTS), tok),
                pl.BlockSpec((tn, N_EXPERTS), tok),
                pl.BlockSpec((1, SLOT_TILE, D_MODEL), win0),
                pl.BlockSpec((1, SLOT_TILE, D_MODEL), win1),
            ],
            out_specs=pl.BlockSpec((tn, D_MODEL), tok),
        ),
        out_shape=jax.ShapeDtypeStruct((n, D_MODEL), F32),
        compiler_params=_params("arbitrary", "arbitrary"),
        name="combine",
    )(lo, hi, x1, slot_t, aff_t, out, out)


def _scatter_kernel(lo_ref, slot_ref, h_ref, z_ref, onehot_ref):
    t = pl.program_id(0)
    sub = lax.broadcasted_iota(jnp.int32, (GATHER_WINDOW, MOE_TILE), 0)
    for e in range(N_EXPERTS):
        rel = slot_ref[e:e + 1, :] - lo_ref[t, e]
        onehot_ref[e * GATHER_WINDOW:(e + 1) * GATHER_WINDOW, :] = (rel == sub).astype(BF16)
    z_ref[0] = jnp.dot(onehot_ref[...], h_ref[...], preferred_element_type=F32).astype(BF16)


def _scatter(lo16, slot2, hf):
    n = hf.shape[0]
    tiles = n // MOE_TILE
    rows = N_EXPERTS * GATHER_WINDOW
    return pl.pallas_call(
        _scatter_kernel,
        grid_spec=pltpu.PrefetchScalarGridSpec(
            num_scalar_prefetch=1,
            grid=(tiles,),
            in_specs=[
                pl.BlockSpec((N_EXPERTS, MOE_TILE), lambda t, lo: (0, t)),
                pl.BlockSpec((MOE_TILE, D_MODEL), lambda t, lo: (t, 0)),
            ],
            out_specs=pl.BlockSpec((1, rows, D_MODEL), lambda t, lo: (t, 0, 0)),
            scratch_shapes=[pltpu.VMEM((rows, MOE_TILE), BF16)],
        ),
        out_shape=jax.ShapeDtypeStruct((tiles, rows, D_MODEL), BF16),
        compiler_params=_params("arbitrary"),
        name="scatter",
    )(lo16, slot2, hf)


def _assemble_kernel(lo_ref, z_ref, xg_ref):
    e = pl.program_id(0)
    xg_ref[...] = jnp.zeros_like(xg_ref)

    def add_tile(t, carry):
        dst = pl.ds(pl.multiple_of(lo_ref[t, e], BF16_ROWS), GATHER_WINDOW)
        xg_ref[0, dst, :] = xg_ref[0, dst, :] + z_ref[t, 0]
        return carry

    lax.fori_loop(0, z_ref.shape[0], add_tile, 0)


def _assemble(lo16, z, capacity):
    tiles = z.shape[0]
    z4 = z.reshape(tiles, N_EXPERTS, GATHER_WINDOW, D_MODEL)
    rows = capacity + CAP_PAD
    return pl.pallas_call(
        _assemble_kernel,
        grid_spec=pltpu.PrefetchScalarGridSpec(
            num_scalar_prefetch=1,
            grid=(N_EXPERTS,),
            in_specs=[pl.BlockSpec((tiles, 1, GATHER_WINDOW, D_MODEL), lambda e, lo: (0, e, 0, 0))],
            out_specs=pl.BlockSpec((1, rows, D_MODEL), lambda e, lo: (e, 0, 0)),
        ),
        out_shape=jax.ShapeDtypeStruct((N_EXPERTS, rows, D_MODEL), BF16),
        compiler_params=_params("arbitrary"),
        name="assemble",
    )(lo16, z4)


def _combine_window_kernel(lo_ref, x1_ref, slot_ref, aff_ref, *rest):
    w_refs, y_ref = rest[:N_EXPERTS], rest[N_EXPERTS]
    t = pl.program_id(0)
    lane = lax.broadcasted_iota(jnp.int32, (MOE_TILE, COMBINE_WINDOW), 1)
    slot = slot_ref[...]
    aff = aff_ref[...]
    y = x1_ref[...]
    for e in range(N_EXPERTS):
        rel = slot[:, e:e + 1] - lo_ref[t, e]
        onehot = (rel == lane).astype(BF16)
        y = y + aff[:, e:e + 1] * jnp.dot(onehot, w_refs[e][0], preferred_element_type=F32)
    y_ref[...] = y


def _combine_window(lo16, x1, slot_t, aff_t, out):
    n = x1.shape[0]
    tok = lambda t, lo: (t, 0)

    def window(e):
        return pl.BlockSpec(
            (pl.Element(1), pl.Element(COMBINE_WINDOW), pl.Element(D_MODEL)),
            lambda t, lo: (e, pl.multiple_of(lo[t, e], BF16_ROWS), 0))

    return pl.pallas_call(
        _combine_window_kernel,
        grid_spec=pltpu.PrefetchScalarGridSpec(
            num_scalar_prefetch=1,
            grid=(n // MOE_TILE,),
            in_specs=[
                pl.BlockSpec((MOE_TILE, D_MODEL), tok),
                pl.BlockSpec((MOE_TILE, N_EXPERTS), tok),
                pl.BlockSpec((MOE_TILE, N_EXPERTS), tok),
            ] + [window(e) for e in range(N_EXPERTS)],
            out_specs=pl.BlockSpec((MOE_TILE, D_MODEL), tok),
        ),
        out_shape=jax.ShapeDtypeStruct((n, D_MODEL), F32),
        compiler_params=_params("arbitrary"),
        name="combine_window",
    )(lo16, x1, slot_t, aff_t, *([out] * N_EXPERTS))


def _head_perms():
    ev = np.arange(0, HEAD_DIM, 2)
    od = np.arange(1, HEAD_DIM, 2)
    return np.concatenate([ev, od]), np.concatenate([od, ev])


def _rope_tables(seq):
    rows = seq // GRID_W
    row = jnp.repeat(jnp.arange(rows, dtype=F32), GRID_W)
    col = jnp.tile(jnp.arange(GRID_W, dtype=F32), rows)
    inv_freq = ROPE_THETA ** (-jnp.arange(0, AXIS_DIM, 2, dtype=F32) / AXIS_DIM)
    ang = jnp.concatenate([row[:, None] * inv_freq, col[:, None] * inv_freq], axis=-1)
    cos, sin = jnp.cos(ang), jnp.sin(ang)
    c_head = jnp.concatenate([cos, cos], axis=-1)
    s_head = jnp.concatenate([-sin, sin], axis=-1)
    scale = HEAD_DIM ** -0.5
    ones_block = jnp.ones((HEAD_DIM, HEAD_DIM), F32)
    return {
        "cq": jnp.tile(c_head, (1, N_HEADS)) * scale,
        "sq": jnp.tile(s_head, (1, N_HEADS)) * scale,
        "ck": jnp.tile(c_head, (1, 2 * N_KV_HEADS)),
        "sk": jnp.tile(s_head, (1, 2 * N_KV_HEADS)),
        "bq": jnp.kron(jnp.eye(N_HEADS, dtype=F32), ones_block).astype(BF16),
        "bk": jnp.kron(jnp.eye(2 * N_KV_HEADS, dtype=F32), ones_block).astype(BF16),
    }


def _layer_weights(l, p):
    perm, perm_sw = _head_perms()
    w_in = p["w_in"][l]
    o_k = ATTN_WIDTH
    o_v = o_k + KV_WIDTH
    o_cv = o_v + KV_WIDTH
    o_cg = o_cv + CONV_WIDTH
    q_cols = np.concatenate([h * HEAD_DIM + perm for h in range(N_HEADS)])
    qs_cols = np.concatenate([h * HEAD_DIM + perm_sw for h in range(N_HEADS)])
    kv_heads = [g for g in range(N_KV_HEADS) for _ in range(2)]
    k_cols = np.concatenate([o_k + g * HEAD_DIM + perm for g in kv_heads])
    ks_cols = np.concatenate([o_k + g * HEAD_DIM + perm_sw for g in kv_heads])
    v_cols = np.concatenate([o_v + g * HEAD_DIM + np.arange(HEAD_DIM) for g in kv_heads])
    cols = np.concatenate([q_cols, qs_cols, k_cols, ks_cols, v_cols,
                           np.arange(o_cv, o_cg), np.arange(o_cg, o_cg + CONV_WIDTH)])
    assert cols.shape[0] == IN_COLS
    row2 = lambda a: a.reshape(1, -1)
    w_router_t = p["w_router"][l].T
    wr_hi, wr_lo = _split_bf16(w_router_t)
    w_out = p["w_out"][l].astype(BF16)
    return {
        "attn_norm_g": row2(p["attn_norm_g"][l]),
        "w_in": w_in[:, cols].astype(BF16),
        "gq": row2(jnp.tile(p["q_norm_g"][l][perm], N_HEADS)),
        "gqs": row2(jnp.tile(p["q_norm_g"][l][perm_sw], N_HEADS)),
        "gk": row2(jnp.tile(p["k_norm_g"][l][perm], 2 * N_KV_HEADS)),
        "gks": row2(jnp.tile(p["k_norm_g"][l][perm_sw], 2 * N_KV_HEADS)),
        "conv_dw_w": p["conv_dw_w"][l],
        "conv_dw_b": row2(p["conv_dw_b"][l]),
        "conv_ln_g": row2(p["conv_ln_g"][l]),
        "conv_ln_b": row2(p["conv_ln_b"][l]),
        "conv_out_g": row2(p["conv_out_g"][l]),
        "attn_out_g": row2(p["attn_out_g"][l]),
        "w_out_a": w_out[:ATTN_WIDTH],
        "w_out_c": w_out[ATTN_WIDTH:],
        "ffn_norm_g": row2(p["ffn_norm_g"][l]),
        "wr_hi": wr_hi,
        "wr_lo": wr_lo,
        "w_gate": p["w_gate"][l],
        "w_up": p["w_up"][l],
        "w_down": p["w_down"][l],
    }


def _moe(x1, hf, aff, lw):
    n = x1.shape[0]
    capacity = CAPACITY_FACTOR * n // N_EXPERTS
    nc = n // LANES
    slot3, off3 = _select(aff.reshape(N_EXPERTS, nc, LANES), capacity)
    bounds = jnp.concatenate([off3[:, :, 0], jnp.full((N_EXPERTS, 1), capacity, jnp.int32)], axis=1)
    slot2 = slot3.reshape(N_EXPERTS, n)
    slot_t = slot2.T
    aff_t = aff.T
    step = MOE_TILE // LANES
    lo = bounds[:, 0:nc:step].T
    hi = bounds[:, step::step].T
    lo16 = (lo // BF16_ROWS) * BF16_ROWS
    fits = jnp.all(hi - lo16 <= GATHER_WINDOW)

    def gather_windowed():
        return _assemble(lo16, _scatter(lo16, slot2, hf), capacity)

    def gather_chunked():
        g_step = GATHER_CHUNK // LANES
        base = bounds[:, 0:nc:g_step]
        end = bounds[:, g_step::g_step]
        rows = slot2.reshape(N_EXPERTS * (n // GATHER_CHUNK), 1, GATHER_CHUNK)
        xg = _gather(base, end, rows, hf, capacity)
        return jnp.pad(xg, ((0, 0), (0, CAP_PAD), (0, 0)))

    xg = lax.cond(fits, gather_windowed, gather_chunked)
    out = _ffn(xg, lw["w_gate"], lw["w_up"], lw["w_down"], capacity)

    def combine_windowed():
        return _combine_window(lo16, x1, slot_t, aff_t, out)

    def combine_chunked():
        c_step = COMBINE_TILE // LANES
        return _combine(bounds[:, 0:nc:c_step].T, bounds[:, c_step::c_step].T, x1, slot_t, aff_t, out)

    return lax.cond(fits, combine_windowed, combine_chunked)


def _layer(x2d, batch, seq, lw, tabs):
    q, k, v, u = _in_proj(x2d, seq, lw, tabs)
    attn = _attention(q.reshape(batch, seq, ATTN_WIDTH), k.reshape(batch, seq, KV_DUP_WIDTH),
                      v.reshape(batch, seq, KV_DUP_WIDTH))
    conv = _conv(u.reshape(batch, seq, CONV_WIDTH), lw)
    x1, hf, aff = _out_proj(x2d, attn.reshape(-1, ATTN_WIDTH), conv.reshape(-1, CONV_WIDTH), lw)
    return _moe(x1, hf, aff, lw)


def _trunk(x, layers):
    batch, seq, d = x.shape
    tabs = _rope_tables(seq)
    x2d = x.reshape(batch * seq, d)
    for lw in layers:
        x2d = _layer(x2d, batch, seq, lw, tabs)
    return x2d.reshape(batch, seq, d)


def kernel(x_prompt, x_sample, attn_norm_g, w_in, q_norm_g, k_norm_g, conv_dw_w, conv_dw_b, conv_ln_g, conv_ln_b, attn_out_g, conv_out_g, w_out, ffn_norm_g, w_router, w_gate, w_up, w_down):
    p = dict(attn_norm_g=attn_norm_g, w_in=w_in, q_norm_g=q_norm_g, k_norm_g=k_norm_g,
             conv_dw_w=conv_dw_w, conv_dw_b=conv_dw_b, conv_ln_g=conv_ln_g, conv_ln_b=conv_ln_b,
             attn_out_g=attn_out_g, conv_out_g=conv_out_g, w_out=w_out, ffn_norm_g=ffn_norm_g,
             w_router=w_router, w_gate=w_gate, w_up=w_up, w_down=w_down)
    layers = [_layer_weights(l, p) for l in range(w_in.shape[0])]
    return (_trunk(x_prompt, layers), _trunk(x_sample, layers))
```

```python
import functools

import numpy as np
import jax
import jax.numpy as jnp
from jax import lax
from jax.experimental import pallas as pl
from jax.experimental.pallas import tpu as pltpu

F32 = jnp.float32
BF16 = jnp.bfloat16

D_MODEL = 1024
HEAD_DIM = 64
N_HEADS = 8
N_KV_HEADS = 2
ATTN_WIDTH = N_HEADS * HEAD_DIM
KV_WIDTH = N_KV_HEADS * HEAD_DIM
KV_DUP_WIDTH = 2 * KV_WIDTH
CONV_WIDTH = D_MODEL - ATTN_WIDTH
CONV_K = 31
GRID_W = 64
ROPE_THETA = 10000.0
AXIS_DIM = HEAD_DIM // 2
N_EXPERTS = 16
EXPERT_FF = 2 * D_MODEL
CAPACITY_FACTOR = 2
EPS = 1e-6

LANES = 128
SUBLANES = 8
BF16_ROWS = 16
VMEM_LIMIT_BYTES = 56 * 1024 * 1024

TOKEN_TILE = 512
Q_TILE = 256
KV_CHUNK = 512
CONV_TILE = 128
CONV_HALO = 16
GATHER_CHUNK = 256
GATHER_SUB = 8
GATHER_WINDOW = GATHER_CHUNK + BF16_ROWS
FF_CHUNK = 512
FFN_ROWS = 512
COMBINE_TILE = LANES
COMBINE_WINDOW = COMBINE_TILE + BF16_ROWS
COMBINE_COLS = 256
CAP_PAD = COMBINE_WINDOW


def _params(*semantics):
    return pltpu.CompilerParams(dimension_semantics=semantics, vmem_limit_bytes=VMEM_LIMIT_BYTES)


def _split_bf16(a):
    hi = a.astype(BF16)
    lo = (a - hi.astype(F32)).astype(BF16)
    return hi, lo


W_Q = (0, 512)
W_QS = (512, 1024)
W_K = (1024, 1280)
W_KS = (1280, 1536)
W_V = (1536, 1792)
W_CV = (1792, 2304)
W_CG = (2304, 2816)
IN_COLS = 2816


def _in_proj_kernel(x_ref, g_ref, w_ref, bq_ref, bk_ref, gq_ref, gqs_ref, gk_ref, gks_ref,
                    cq_ref, sq_ref, ck_ref, sk_ref, q_out, k_out, v_out, u_out):
    x = x_ref[...]
    ms = jnp.mean(x * x, axis=-1, keepdims=True)
    h = (x * lax.rsqrt(ms + EPS) * g_ref[...]).astype(BF16)
    proj = jnp.dot(h, w_ref[...], preferred_element_type=F32)

    def head_rsqrt(a, ones_blockdiag):
        hi, lo = _split_bf16(a * a)
        ss = (jnp.dot(hi, ones_blockdiag, preferred_element_type=F32)
              + jnp.dot(lo, ones_blockdiag, preferred_element_type=F32))
        return lax.rsqrt(ss * (1.0 / HEAD_DIM) + EPS)

    q = proj[:, W_Q[0]:W_Q[1]]
    qs = proj[:, W_QS[0]:W_QS[1]]
    rq = head_rsqrt(q, bq_ref[...])
    q_out[...] = ((q * rq * gq_ref[...]) * cq_ref[...]
                  + (qs * rq * gqs_ref[...]) * sq_ref[...]).astype(BF16)

    k = proj[:, W_K[0]:W_K[1]]
    ks = proj[:, W_KS[0]:W_KS[1]]
    rk = head_rsqrt(k, bk_ref[...])
    k_out[...] = ((k * rk * gk_ref[...]) * ck_ref[...]
                  + (ks * rk * gks_ref[...]) * sk_ref[...]).astype(BF16)

    v_out[...] = proj[:, W_V[0]:W_V[1]].astype(BF16)
    cv = proj[:, W_CV[0]:W_CV[1]]
    cg = proj[:, W_CG[0]:W_CG[1]]
    u_out[...] = cv * jax.nn.sigmoid(cg)


def _in_proj(x2d, seq, lw, tabs):
    n = x2d.shape[0]
    tm = TOKEN_TILE
    seq_tiles = seq // tm
    row = lambda i: (i, 0)
    fixed = lambda i: (0, 0)
    pos = lambda i: (i % seq_tiles, 0)
    full = lambda a: pl.BlockSpec(a.shape, fixed)
    return pl.pallas_call(
        _in_proj_kernel,
        grid=(n // tm,),
        in_specs=[
            pl.BlockSpec((tm, D_MODEL), row),
            full(lw["attn_norm_g"]), full(lw["w_in"]), full(tabs["bq"]), full(tabs["bk"]),
            full(lw["gq"]), full(lw["gqs"]), full(lw["gk"]), full(lw["gks"]),
            pl.BlockSpec((tm, ATTN_WIDTH), pos), pl.BlockSpec((tm, ATTN_WIDTH), pos),
            pl.BlockSpec((tm, KV_DUP_WIDTH), pos), pl.BlockSpec((tm, KV_DUP_WIDTH), pos),
        ],
        out_specs=[
            pl.BlockSpec((tm, ATTN_WIDTH), row),
            pl.BlockSpec((tm, KV_DUP_WIDTH), row),
            pl.BlockSpec((tm, KV_DUP_WIDTH), row),
            pl.BlockSpec((tm, CONV_WIDTH), row),
        ],
        out_shape=[
            jax.ShapeDtypeStruct((n, ATTN_WIDTH), BF16),
            jax.ShapeDtypeStruct((n, KV_DUP_WIDTH), BF16),
            jax.ShapeDtypeStruct((n, KV_DUP_WIDTH), BF16),
            jax.ShapeDtypeStruct((n, CONV_WIDTH), F32),
        ],
        compiler_params=_params("arbitrary"),
        name="in_proj",
    )(x2d, lw["attn_norm_g"], lw["w_in"], tabs["bq"], tabs["bk"],
      lw["gq"], lw["gqs"], lw["gk"], lw["gks"],
      tabs["cq"], tabs["sq"], tabs["ck"], tabs["sk"])


def _attention_kernel(q_ref, k_ref, v_ref, o_ref):
    q = q_ref[0]
    tq = q.shape[0]
    lane = lax.broadcasted_iota(jnp.int32, q.shape, 1)
    first = lane < HEAD_DIM
    zero = jnp.zeros_like(q)
    q2 = jnp.concatenate([jnp.where(first, q, zero), jnp.where(first, zero, q)], axis=0)
    seq = k_ref.shape[1]
    m = jnp.full((2 * tq, 1), -jnp.inf, F32)
    l = jnp.zeros((2 * tq, 1), F32)
    acc = jnp.zeros((2 * tq, LANES), F32)
    for c in range(seq // KV_CHUNK):
        keys = pl.ds(c * KV_CHUNK, KV_CHUNK)
        s = lax.dot_general(q2, k_ref[0, keys, :], (((1,), (1,)), ((), ())), preferred_element_type=F32)
        m_new = jnp.maximum(m, jnp.max(s, axis=-1, keepdims=True))
        alpha = jnp.exp(m - m_new)
        p = jnp.exp(s - m_new)
        l = alpha * l + jnp.sum(p, axis=-1, keepdims=True)
        acc = alpha * acc + jnp.dot(p.astype(BF16), v_ref[0, keys, :], preferred_element_type=F32)
        m = m_new
    o2 = acc / l
    o_ref[0] = jnp.where(first, o2[:tq], o2[tq:])


def _attention(q, k, v):
    b, s, _ = q.shape
    pairs = N_HEADS // 2
    pairs_per_kv = pairs // N_KV_HEADS
    return pl.pallas_call(
        _attention_kernel,
        grid=(b, pairs, s // Q_TILE),
        in_specs=[
            pl.BlockSpec((1, Q_TILE, LANES), lambda bi, j, i: (bi, i, j)),
            pl.BlockSpec((1, s, LANES), lambda bi, j, i: (bi, 0, j // pairs_per_kv)),
            pl.BlockSpec((1, s, LANES), lambda bi, j, i: (bi, 0, j // pairs_per_kv)),
        ],
        out_specs=pl.BlockSpec((1, Q_TILE, LANES), lambda bi, j, i: (bi, i, j)),
        out_shape=jax.ShapeDtypeStruct((b, s, ATTN_WIDTH), F32),
        compiler_params=_params("arbitrary", "arbitrary", "arbitrary"),
        name="attention",
    )(q, k, v)


def _conv_kernel(up_ref, uc_ref, un_ref, w_ref, b_ref, lng_ref, lnb_ref, og_ref, o_ref):
    i = pl.program_id(1)
    last = pl.num_programs(1) - 1
    prev = jnp.where(i > 0, up_ref[0], 0.0)
    nxt = jnp.where(i < last, un_ref[0], 0.0)
    win = jnp.concatenate([prev, uc_ref[0], nxt], axis=0)
    tr = uc_ref.shape[1]
    w = w_ref[...]
    lead = CONV_HALO - CONV_K // 2
    shifted_len = tr + 2 * CONV_HALO - SUBLANES
    acc = jnp.zeros((tr, CONV_WIDTH), F32) + b_ref[...]
    for rho in range(SUBLANES):
        taps = [k for k in range(CONV_K) if (k + lead) % SUBLANES == rho]
        if not taps:
            continue
        sh = win[rho:rho + shifted_len]
        for k in taps:
            a = (k + lead) - rho
            acc = acc + sh[a:a + tr] * w[k:k + 1, :]
    mu = jnp.mean(acc, axis=-1, keepdims=True)
    xc = acc - mu
    y = xc * lax.rsqrt(jnp.mean(xc * xc, axis=-1, keepdims=True) + EPS) * lng_ref[...] + lnb_ref[...]
    y = y * jax.nn.sigmoid(y)
    z = y * lax.rsqrt(jnp.mean(y * y, axis=-1, keepdims=True) + EPS) * og_ref[...]
    o_ref[0] = z.astype(BF16)


def _conv(u, lw):
    b, s, c = u.shape
    tr = CONV_TILE
    halo_per_tile = tr // CONV_HALO
    n_halo = s // CONV_HALO
    fixed = lambda bi, i: (0, 0)
    full = lambda a: pl.BlockSpec(a.shape, fixed)
    return pl.pallas_call(
        _conv_kernel,
        grid=(b, s // tr),
        in_specs=[
            pl.BlockSpec((1, CONV_HALO, c), lambda bi, i: (bi, jnp.maximum(i * halo_per_tile - 1, 0), 0)),
            pl.BlockSpec((1, tr, c), lambda bi, i: (bi, i, 0)),
            pl.BlockSpec((1, CONV_HALO, c),
                         lambda bi, i: (bi, jnp.minimum((i + 1) * halo_per_tile, n_halo - 1), 0)),
            full(lw["conv_dw_w"]), full(lw["conv_dw_b"]), full(lw["conv_ln_g"]),
            full(lw["conv_ln_b"]), full(lw["conv_out_g"]),
        ],
        out_specs=pl.BlockSpec((1, tr, c), lambda bi, i: (bi, i, 0)),
        out_shape=jax.ShapeDtypeStruct((b, s, c), BF16),
        compiler_params=_params("arbitrary", "arbitrary"),
        name="conv",
    )(u, u, u, lw["conv_dw_w"], lw["conv_dw_b"], lw["conv_ln_g"], lw["conv_ln_b"], lw["conv_out_g"])


def _out_proj_kernel(x_ref, a_ref, c_ref, ag_ref, wa_ref, wc_ref, fg_ref, wrh_ref, wrl_ref,
                     x1_out, hf_out, aff_out):
    a = a_ref[...]
    an = (a * lax.rsqrt(jnp.mean(a * a, axis=-1, keepdims=True) + EPS) * ag_ref[...]).astype(BF16)
    x1 = (x_ref[...]
          + jnp.dot(an, wa_ref[...], preferred_element_type=F32)
          + jnp.dot(c_ref[...], wc_ref[...], preferred_element_type=F32))
    x1_out[...] = x1
    hf = x1 * lax.rsqrt(jnp.mean(x1 * x1, axis=-1, keepdims=True) + EPS) * fg_ref[...]
    hf_hi, hf_lo = _split_bf16(hf)
    hf_out[...] = hf_hi
    nt = (((1,), (1,)), ((), ()))
    logits = (lax.dot_general(wrh_ref[...], hf_hi, nt, preferred_element_type=F32)
              + lax.dot_general(wrh_ref[...], hf_lo, nt, preferred_element_type=F32)
              + lax.dot_general(wrl_ref[...], hf_hi, nt, preferred_element_type=F32))
    mx = jnp.max(logits, axis=0, keepdims=True)
    ex = jnp.exp(logits - mx)
    aff_out[...] = ex / jnp.sum(ex, axis=0, keepdims=True)


def _out_proj(x2d, attn2d, conv2d, lw):
    n = x2d.shape[0]
    tm = TOKEN_TILE
    row = lambda i: (i, 0)
    fixed = lambda i: (0, 0)
    full = lambda a: pl.BlockSpec(a.shape, fixed)
    return pl.pallas_call(
        _out_proj_kernel,
        grid=(n // tm,),
        in_specs=[
            pl.BlockSpec((tm, D_MODEL), row),
            pl.BlockSpec((tm, ATTN_WIDTH), row),
            pl.BlockSpec((tm, CONV_WIDTH), row),
            full(lw["attn_out_g"]), full(lw["w_out_a"]), full(lw["w_out_c"]), full(lw["ffn_norm_g"]),
            full(lw["wr_hi"]), full(lw["wr_lo"]),
        ],
        out_specs=[
            pl.BlockSpec((tm, D_MODEL), row),
            pl.BlockSpec((tm, D_MODEL), row),
            pl.BlockSpec((N_EXPERTS, tm), lambda i: (0, i)),
        ],
        out_shape=[
            jax.ShapeDtypeStruct((n, D_MODEL), F32),
            jax.ShapeDtypeStruct((n, D_MODEL), BF16),
            jax.ShapeDtypeStruct((N_EXPERTS, n), F32),
        ],
        compiler_params=_params("arbitrary"),
        name="out_proj",
    )(x2d, attn2d, conv2d, lw["attn_out_g"], lw["w_out_a"], lw["w_out_c"], lw["ffn_norm_g"],
      lw["wr_hi"], lw["wr_lo"])


def _select_kernel(aff_ref, slot_out, off_out, *, capacity):
    a = aff_ref[...]
    e, nc, _ = a.shape
    keys = lax.bitcast_convert_type(a, jnp.int32)

    def count(pred):
        c = jnp.sum(pred.astype(F32), axis=1, keepdims=True)
        return jnp.sum(c, axis=2, keepdims=True)

    cap = float(capacity)

    def bit_step(i, t):
        cand = t | jnp.left_shift(jnp.int32(1), 30 - i)
        return jnp.where(count(keys >= cand) >= cap, cand, t)

    thr = lax.fori_loop(0, 31, bit_step, jnp.zeros((e, 1, 1), jnp.int32))
    above = keys > thr
    tied = keys == thr
    need = cap - count(above)

    li = lax.broadcasted_iota(jnp.int32, (LANES, LANES), 0)
    lj = lax.broadcasted_iota(jnp.int32, (LANES, LANES), 1)
    before_lane = (li < lj).astype(BF16)
    ci = lax.broadcasted_iota(jnp.int32, (nc, nc), 0)
    cj = lax.broadcasted_iota(jnp.int32, (nc, nc), 1)
    before_chunk = jnp.broadcast_to((cj < ci).astype(BF16)[None], (e, nc, nc))

    def prefix(pred):
        mb = pred.astype(BF16)
        inchunk = jnp.dot(mb.reshape(e * nc, LANES), before_lane,
                          preferred_element_type=F32).reshape(e, nc, LANES)
        part = lax.dot_general(before_chunk, mb, (((2,), (1,)), ((0,), (0,))),
                               preferred_element_type=F32)
        off = jnp.sum(part, axis=2, keepdims=True)
        return inchunk + off, off

    tie_rank, _ = prefix(tied)
    mask = above | (tied & (tie_rank < need))
    rank, off = prefix(mask)
    slot_out[...] = jnp.where(mask, rank, -1.0).astype(jnp.int32)
    off_out[...] = jnp.broadcast_to(off, off_out.shape).astype(jnp.int32)


def _select(aff3, capacity):
    e, nc, _ = aff3.shape
    return pl.pallas_call(
        functools.partial(_select_kernel, capacity=capacity),
        out_shape=[
            jax.ShapeDtypeStruct((e, nc, LANES), jnp.int32),
            jax.ShapeDtypeStruct((e, nc, LANES), jnp.int32),
        ],
        compiler_params=pltpu.CompilerParams(vmem_limit_bytes=VMEM_LIMIT_BYTES),
        name="select",
    )(aff3)


def _gather_kernel(bnd_ref, slot_ref, h_ref, xg_ref):
    e = pl.program_id(0)
    s = pl.program_id(1)

    @pl.when(s == 0)
    def _():
        xg_ref[...] = jnp.zeros_like(xg_ref)

    sub = lax.broadcasted_iota(jnp.int32, (GATHER_WINDOW, GATHER_CHUNK), 0)
    for ci in range(GATHER_SUB):
        start = (bnd_ref[e, s * GATHER_SUB + ci] // BF16_ROWS) * BF16_ROWS
        srow = slot_ref[0, ci:ci + 1, :]
        onehot = (srow - start == sub).astype(BF16)
        h = h_ref[ci * GATHER_CHUNK:(ci + 1) * GATHER_CHUNK, :]
        rows = jnp.dot(onehot, h, preferred_element_type=F32).astype(BF16)
        dst = pl.ds(pl.multiple_of(start, BF16_ROWS), GATHER_WINDOW)
        xg_ref[0, dst, :] = xg_ref[0, dst, :] + rows


def _gather(bounds, slot2, hf, capacity):
    n = hf.shape[0]
    chunks = n // GATHER_CHUNK
    rows = GATHER_SUB * GATHER_CHUNK
    padded = capacity + GATHER_WINDOW
    return pl.pallas_call(
        _gather_kernel,
        grid_spec=pltpu.PrefetchScalarGridSpec(
            num_scalar_prefetch=1,
            grid=(N_EXPERTS, chunks // GATHER_SUB),
            in_specs=[
                pl.BlockSpec((1, GATHER_SUB, GATHER_CHUNK), lambda e, s, b_: (e, s, 0)),
                pl.BlockSpec((rows, D_MODEL), lambda e, s, b_: (s, 0)),
            ],
            out_specs=pl.BlockSpec((1, padded, D_MODEL), lambda e, s, b_: (e, 0, 0)),
        ),
        out_shape=jax.ShapeDtypeStruct((N_EXPERTS, padded, D_MODEL), BF16),
        compiler_params=_params("arbitrary", "arbitrary"),
        name="gather",
    )(bounds, slot2.reshape(N_EXPERTS, chunks, GATHER_CHUNK), hf)


def _ffn_kernel(xg_ref, wg_ref, wu_ref, wd_ref, o_ref, acc_ref):
    f = pl.program_id(1)

    @pl.when(f == 0)
    def _():
        acc_ref[...] = jnp.zeros_like(acc_ref)

    wg = wg_ref[0].astype(BF16)
    wu = wu_ref[0].astype(BF16)
    wd = wd_ref[0].astype(BF16)
    cap = acc_ref.shape[0]
    block = min(FFN_ROWS, cap)
    for r in range(cap // block):
        rows = pl.ds(r * block, block)
        x = xg_ref[0, rows, :]
        a = jnp.dot(x, wg, preferred_element_type=F32)
        b = jnp.dot(x, wu, preferred_element_type=F32)
        h = (a * jax.nn.sigmoid(a) * b).astype(BF16)
        acc_ref[rows, :] += jnp.dot(h, wd, preferred_element_type=F32)

    @pl.when(f == pl.num_programs(1) - 1)
    def _():
        o_ref[0, pl.ds(0, cap), :] = acc_ref[...].astype(BF16)
        o_ref[0, pl.ds(cap, CAP_PAD), :] = jnp.zeros((CAP_PAD, o_ref.shape[2]), BF16)


def _ffn(xg, w_gate, w_up, w_down, capacity):
    e, _, d = xg.shape
    return pl.pallas_call(
        _ffn_kernel,
        grid=(e, EXPERT_FF // FF_CHUNK),
        in_specs=[
            pl.BlockSpec((1, capacity, d), lambda ei, f: (ei, 0, 0)),
            pl.BlockSpec((1, d, FF_CHUNK), lambda ei, f: (ei, 0, f)),
            pl.BlockSpec((1, d, FF_CHUNK), lambda ei, f: (ei, 0, f)),
            pl.BlockSpec((1, FF_CHUNK, d), lambda ei, f: (ei, f, 0)),
        ],
        out_specs=pl.BlockSpec((1, capacity + CAP_PAD, d), lambda ei, f: (ei, 0, 0)),
        out_shape=jax.ShapeDtypeStruct((e, capacity + CAP_PAD, d), BF16),
        scratch_shapes=[pltpu.VMEM((capacity, d), F32)],
        compiler_params=_params("arbitrary", "arbitrary"),
        name="ffn",
    )(xg, w_gate, w_up, w_down)


def _combine_kernel(lo_ref, x1_ref, slot_ref, aff_ref, *rest):
    w_refs, y_ref = rest[:N_EXPERTS], rest[N_EXPERTS]
    t = pl.program_id(0)
    lane = lax.broadcasted_iota(jnp.int32, (COMBINE_TILE, COMBINE_WINDOW), 1)
    slot = slot_ref[...]
    aff = aff_ref[...]
    onehots = []
    for e in range(N_EXPERTS):
        rel = slot[:, e:e + 1] - lo_ref[t, e]
        onehots.append((rel == lane).astype(BF16))
    for cb in range(D_MODEL // COMBINE_COLS):
        cols = pl.ds(cb * COMBINE_COLS, COMBINE_COLS)
        y = x1_ref[:, cols]
        for e in range(N_EXPERTS):
            y = y + aff[:, e:e + 1] * jnp.dot(onehots[e], w_refs[e][0, :, cols],
                                              preferred_element_type=F32)
        y_ref[:, cols] = y


def _combine(lo_aligned, x1, slot_t, aff_t, out):
    n = x1.shape[0]
    tok = lambda t, lo: (t, 0)

    def window(e):
        return pl.BlockSpec(
            (pl.Element(1), pl.Element(COMBINE_WINDOW), pl.Element(D_MODEL)),
            lambda t, lo: (e, pl.multiple_of(lo[t, e], BF16_ROWS), 0))

    return pl.pallas_call(
        _combine_kernel,
        grid_spec=pltpu.PrefetchScalarGridSpec(
            num_scalar_prefetch=1,
            grid=(n // COMBINE_TILE,),
            in_specs=[
                pl.BlockSpec((COMBINE_TILE, D_MODEL), tok),
                pl.BlockSpec((COMBINE_TILE, N_EXPERTS), tok),
                pl.BlockSpec((COMBINE_TILE, N_EXPERTS), tok),
            ] + [window(e) for e in range(N_EXPERTS)],
            out_specs=pl.BlockSpec((COMBINE_TILE, D_MODEL), tok),
        ),
        out_shape=jax.ShapeDtypeStruct((n, D_MODEL), F32),
        compiler_params=_params("arbitrary"),
        name="combine",
    )(lo_aligned, x1, slot_t, aff_t, *([out] * N_EXPERTS))


def _head_perms():
    ev = np.arange(0, HEAD_DIM, 2)
    od = np.arange(1, HEAD_DIM, 2)
    return np.concatenate([ev, od]), np.concatenate([od, ev])


def _rope_tables(seq):
    rows = seq // GRID_W
    row = jnp.repeat(jnp.arange(rows, dtype=F32), GRID_W)
    col = jnp.tile(jnp.arange(GRID_W, dtype=F32), rows)
    inv_freq = ROPE_THETA ** (-jnp.arange(0, AXIS_DIM, 2, dtype=F32) / AXIS_DIM)
    ang = jnp.concatenate([row[:, None] * inv_freq, col[:, None] * inv_freq], axis=-1)
    cos, sin = jnp.cos(ang), jnp.sin(ang)
    c_head = jnp.concatenate([cos, cos], axis=-1)
    s_head = jnp.concatenate([-sin, sin], axis=-1)
    scale = HEAD_DIM ** -0.5
    ones_block = jnp.ones((HEAD_DIM, HEAD_DIM), F32)
    return {
        "cq": jnp.tile(c_head, (1, N_HEADS)) * scale,
        "sq": jnp.tile(s_head, (1, N_HEADS)) * scale,
        "ck": jnp.tile(c_head, (1, 2 * N_KV_HEADS)),
        "sk": jnp.tile(s_head, (1, 2 * N_KV_HEADS)),
        "bq": jnp.kron(jnp.eye(N_HEADS, dtype=F32), ones_block).astype(BF16),
        "bk": jnp.kron(jnp.eye(2 * N_KV_HEADS, dtype=F32), ones_block).astype(BF16),
    }


def _layer_weights(l, p):
    perm, perm_sw = _head_perms()
    w_in = p["w_in"][l]
    o_k = ATTN_WIDTH
    o_v = o_k + KV_WIDTH
    o_cv = o_v + KV_WIDTH
    o_cg = o_cv + CONV_WIDTH
    q_cols = np.concatenate([h * HEAD_DIM + perm for h in range(N_HEADS)])
    qs_cols = np.concatenate([h * HEAD_DIM + perm_sw for h in range(N_HEADS)])
    kv_heads = [g for g in range(N_KV_HEADS) for _ in range(2)]
    k_cols = np.concatenate([o_k + g * HEAD_DIM + perm for g in kv_heads])
    ks_cols = np.concatenate([o_k + g * HEAD_DIM + perm_sw for g in kv_heads])
    v_cols = np.concatenate([o_v + g * HEAD_DIM + np.arange(HEAD_DIM) for g in kv_heads])
    cols = np.concatenate([q_cols, qs_cols, k_cols, ks_cols, v_cols,
                           np.arange(o_cv, o_cg), np.arange(o_cg, o_cg + CONV_WIDTH)])
    assert cols.shape[0] == IN_COLS
    row2 = lambda a: a.reshape(1, -1)
    w_router_t = p["w_router"][l].T
    wr_hi, wr_lo = _split_bf16(w_router_t)
    w_out = p["w_out"][l].astype(BF16)
    return {
        "attn_norm_g": row2(p["attn_norm_g"][l]),
        "w_in": w_in[:, cols].astype(BF16),
        "gq": row2(jnp.tile(p["q_norm_g"][l][perm], N_HEADS)),
        "gqs": row2(jnp.tile(p["q_norm_g"][l][perm_sw], N_HEADS)),
        "gk": row2(jnp.tile(p["k_norm_g"][l][perm], 2 * N_KV_HEADS)),
        "gks": row2(jnp.tile(p["k_norm_g"][l][perm_sw], 2 * N_KV_HEADS)),
        "conv_dw_w": p["conv_dw_w"][l],
        "conv_dw_b": row2(p["conv_dw_b"][l]),
        "conv_ln_g": row2(p["conv_ln_g"][l]),
        "conv_ln_b": row2(p["conv_ln_b"][l]),
        "conv_out_g": row2(p["conv_out_g"][l]),
        "attn_out_g": row2(p["attn_out_g"][l]),
        "w_out_a": w_out[:ATTN_WIDTH],
        "w_out_c": w_out[ATTN_WIDTH:],
        "ffn_norm_g": row2(p["ffn_norm_g"][l]),
        "wr_hi": wr_hi,
        "wr_lo": wr_lo,
        "w_gate": p["w_gate"][l],
        "w_up": p["w_up"][l],
        "w_down": p["w_down"][l],
    }


def _moe(x1, hf, aff, lw):
    n = x1.shape[0]
    capacity = CAPACITY_FACTOR * n // N_EXPERTS
    nc = n // LANES
    slot3, off3 = _select(aff.reshape(N_EXPERTS, nc, LANES), capacity)
    bounds = jnp.concatenate([off3[:, :, 0], jnp.full((N_EXPERTS, 1), capacity, jnp.int32)], axis=1)
    slot2 = slot3.reshape(N_EXPERTS, n)
    xg = _gather(bounds[:, ::GATHER_CHUNK // LANES], slot2, hf, capacity)
    out = _ffn(xg, lw["w_gate"], lw["w_up"], lw["w_down"], capacity)
    lo = bounds[:, :nc].T
    return _combine((lo // BF16_ROWS) * BF16_ROWS, x1, slot2.T, aff.T, out)


def _layer(x2d, batch, seq, lw, tabs):
    q, k, v, u = _in_proj(x2d, seq, lw, tabs)
    attn = _attention(q.reshape(batch, seq, ATTN_WIDTH), k.reshape(batch, seq, KV_DUP_WIDTH),
                      v.reshape(batch, seq, KV_DUP_WIDTH))
    conv = _conv(u.reshape(batch, seq, CONV_WIDTH), lw)
    x1, hf, aff = _out_proj(x2d, attn.reshape(-1, ATTN_WIDTH), conv.reshape(-1, CONV_WIDTH), lw)
    return _moe(x1, hf, aff, lw)


def _trunk(x, layers):
    batch, seq, d = x.shape
    tabs = _rope_tables(seq)
    x2d = x.reshape(batch * seq, d)
    for lw in layers:
        x2d = _layer(x2d, batch, seq, lw, tabs)
    return x2d.reshape(batch, seq, d)


def kernel(x_prompt, x_sample, attn_norm_g, w_in, q_norm_g, k_norm_g, conv_dw_w, conv_dw_b, conv_ln_g, conv_ln_b, attn_out_g, conv_out_g, w_out, ffn_norm_g, w_router, w_gate, w_up, w_down):
    p = dict(attn_norm_g=attn_norm_g, w_in=w_in, q_norm_g=q_norm_g, k_norm_g=k_norm_g,
             conv_dw_w=conv_dw_w, conv_dw_b=conv_dw_b, conv_ln_g=conv_ln_g, conv_ln_b=conv_ln_b,
             attn_out_g=attn_out_g, conv_out_g=conv_out_g, w_out=w_out, ffn_norm_g=ffn_norm_g,
             w_router=w_router, w_gate=w_gate, w_up=w_up, w_down=w_down)
    layers = [_layer_weights(l, p) for l in range(w_in.shape[0])]
    return (_trunk(x_prompt, layers), _trunk(x_sample, layers))
```

```python
import functools

import numpy as np
import jax
import jax.numpy as jnp
from jax import lax
from jax.experimental import pallas as pl
from jax.experimental.pallas import tpu as pltpu

F32 = jnp.float32
BF16 = jnp.bfloat16

D_MODEL = 1024
HEAD_DIM = 64
N_HEADS = 8
N_KV_HEADS = 2
ATTN_WIDTH = N_HEADS * HEAD_DIM
KV_WIDTH = N_KV_HEADS * HEAD_DIM
KV_DUP_WIDTH = 2 * KV_WIDTH
CONV_WIDTH = D_MODEL - ATTN_WIDTH
CONV_K = 31
GRID_W = 64
ROPE_THETA = 10000.0
AXIS_DIM = HEAD_DIM // 2
N_EXPERTS = 16
EXPERT_FF = 2 * D_MODEL
CAPACITY_FACTOR = 2
EPS = 1e-6

LANES = 128
SUBLANES = 8
BF16_ROWS = 16
VMEM_LIMIT_BYTES = 56 * 1024 * 1024

TOKEN_TILE = 512
Q_TILE = 512
KV_CHUNK = 1024
CONV_TILE = 128
CONV_HALO = 16
GATHER_CHUNK = 256
GATHER_SUB = 8
GATHER_EXPERTS = 2
GATHER_WINDOW = GATHER_CHUNK + BF16_ROWS
FF_CHUNK = 512
FFN_ROWS = 512
COMBINE_TILE = LANES
COMBINE_WINDOW = COMBINE_TILE + BF16_ROWS
COMBINE_COLS = 256
CAP_PAD = COMBINE_WINDOW


def _params(*semantics):
    return pltpu.CompilerParams(dimension_semantics=semantics, vmem_limit_bytes=VMEM_LIMIT_BYTES)


def _split_bf16(a):
    hi = a.astype(BF16)
    lo = (a - hi.astype(F32)).astype(BF16)
    return hi, lo


W_Q = (0, 512)
W_QS = (512, 1024)
W_K = (1024, 1280)
W_KS = (1280, 1536)
W_V = (1536, 1792)
W_CV = (1792, 2304)
W_CG = (2304, 2816)
IN_COLS = 2816


def _in_proj_kernel(x_ref, g_ref, w_ref, bq_ref, bk_ref, gq_ref, gqs_ref, gk_ref, gks_ref,
                    cq_ref, sq_ref, ck_ref, sk_ref, q_out, k_out, v_out, u_out):
    x = x_ref[...]
    ms = jnp.mean(x * x, axis=-1, keepdims=True)
    h = (x * lax.rsqrt(ms + EPS) * g_ref[...]).astype(BF16)
    proj = jnp.dot(h, w_ref[...], preferred_element_type=F32)

    def head_rsqrt(a, ones_blockdiag):
        hi, lo = _split_bf16(a * a)
        ss = (jnp.dot(hi, ones_blockdiag, preferred_element_type=F32)
              + jnp.dot(lo, ones_blockdiag, preferred_element_type=F32))
        return lax.rsqrt(ss * (1.0 / HEAD_DIM) + EPS)

    q = proj[:, W_Q[0]:W_Q[1]]
    qs = proj[:, W_QS[0]:W_QS[1]]
    rq = head_rsqrt(q, bq_ref[...])
    q_out[...] = ((q * rq * gq_ref[...]) * cq_ref[...]
                  + (qs * rq * gqs_ref[...]) * sq_ref[...]).astype(BF16)

    k = proj[:, W_K[0]:W_K[1]]
    ks = proj[:, W_KS[0]:W_KS[1]]
    rk = head_rsqrt(k, bk_ref[...])
    k_out[...] = ((k * rk * gk_ref[...]) * ck_ref[...]
                  + (ks * rk * gks_ref[...]) * sk_ref[...]).astype(BF16)

    v_out[...] = proj[:, W_V[0]:W_V[1]].astype(BF16)
    cv = proj[:, W_CV[0]:W_CV[1]]
    cg = proj[:, W_CG[0]:W_CG[1]]
    u_out[...] = cv * jax.nn.sigmoid(cg)


def _in_proj(x2d, seq, lw, tabs):
    n = x2d.shape[0]
    tm = TOKEN_TILE
    seq_tiles = seq // tm
    row = lambda i: (i, 0)
    fixed = lambda i: (0, 0)
    pos = lambda i: (i % seq_tiles, 0)
    full = lambda a: pl.BlockSpec(a.shape, fixed)
    return pl.pallas_call(
        _in_proj_kernel,
        grid=(n // tm,),
        in_specs=[
            pl.BlockSpec((tm, D_MODEL), row),
            full(lw["attn_norm_g"]), full(lw["w_in"]), full(tabs["bq"]), full(tabs["bk"]),
            full(lw["gq"]), full(lw["gqs"]), full(lw["gk"]), full(lw["gks"]),
            pl.BlockSpec((tm, ATTN_WIDTH), pos), pl.BlockSpec((tm, ATTN_WIDTH), pos),
            pl.BlockSpec((tm, KV_DUP_WIDTH), pos), pl.BlockSpec((tm, KV_DUP_WIDTH), pos),
        ],
        out_specs=[
            pl.BlockSpec((tm, ATTN_WIDTH), row),
            pl.BlockSpec((tm, KV_DUP_WIDTH), row),
            pl.BlockSpec((tm, KV_DUP_WIDTH), row),
            pl.BlockSpec((tm, CONV_WIDTH), row),
        ],
        out_shape=[
            jax.ShapeDtypeStruct((n, ATTN_WIDTH), BF16),
            jax.ShapeDtypeStruct((n, KV_DUP_WIDTH), BF16),
            jax.ShapeDtypeStruct((n, KV_DUP_WIDTH), BF16),
            jax.ShapeDtypeStruct((n, CONV_WIDTH), F32),
        ],
        compiler_params=_params("arbitrary"),
        name="in_proj",
    )(x2d, lw["attn_norm_g"], lw["w_in"], tabs["bq"], tabs["bk"],
      lw["gq"], lw["gqs"], lw["gk"], lw["gks"],
      tabs["cq"], tabs["sq"], tabs["ck"], tabs["sk"])


def _attention_kernel(q_ref, k_ref, v_ref, o_ref):
    q = q_ref[0]
    tq = q.shape[0]
    lane = lax.broadcasted_iota(jnp.int32, q.shape, 1)
    first = lane < HEAD_DIM
    zero = jnp.zeros_like(q)
    q2 = jnp.concatenate([jnp.where(first, q, zero), jnp.where(first, zero, q)], axis=0)
    seq = k_ref.shape[1]
    m = jnp.full((2 * tq, 1), -jnp.inf, F32)
    l = jnp.zeros((2 * tq, 1), F32)
    acc = jnp.zeros((2 * tq, LANES), F32)
    for c in range(seq // KV_CHUNK):
        keys = pl.ds(c * KV_CHUNK, KV_CHUNK)
        s = lax.dot_general(q2, k_ref[0, keys, :], (((1,), (1,)), ((), ())), preferred_element_type=F32)
        m_new = jnp.maximum(m, jnp.max(s, axis=-1, keepdims=True))
        alpha = jnp.exp(m - m_new)
        p = jnp.exp(s - m_new)
        l = alpha * l + jnp.sum(p, axis=-1, keepdims=True)
        acc = alpha * acc + jnp.dot(p.astype(BF16), v_ref[0, keys, :], preferred_element_type=F32)
        m = m_new
    o2 = acc / l
    o_ref[0] = jnp.where(first, o2[:tq], o2[tq:])


def _attention(q, k, v):
    b, s, _ = q.shape
    pairs = N_HEADS // 2
    pairs_per_kv = pairs // N_KV_HEADS
    return pl.pallas_call(
        _attention_kernel,
        grid=(b, pairs, s // Q_TILE),
        in_specs=[
            pl.BlockSpec((1, Q_TILE, LANES), lambda bi, j, i: (bi, i, j)),
            pl.BlockSpec((1, s, LANES), lambda bi, j, i: (bi, 0, j // pairs_per_kv)),
            pl.BlockSpec((1, s, LANES), lambda bi, j, i: (bi, 0, j // pairs_per_kv)),
        ],
        out_specs=pl.BlockSpec((1, Q_TILE, LANES), lambda bi, j, i: (bi, i, j)),
        out_shape=jax.ShapeDtypeStruct((b, s, ATTN_WIDTH), F32),
        compiler_params=_params("arbitrary", "arbitrary", "arbitrary"),
        name="attention",
    )(q, k, v)


def _conv_kernel(up_ref, uc_ref, un_ref, w_ref, b_ref, lng_ref, lnb_ref, og_ref, o_ref, sh_ref):
    i = pl.program_id(1)
    last = pl.num_programs(1) - 1
    prev = jnp.where(i > 0, up_ref[0], 0.0)
    nxt = jnp.where(i < last, un_ref[0], 0.0)
    win = jnp.concatenate([prev, uc_ref[0], nxt], axis=0)
    tr = uc_ref.shape[1]
    w = w_ref[...]
    lead = CONV_HALO - CONV_K // 2
    shifted_len = tr + 2 * CONV_HALO - SUBLANES
    for rho in range(SUBLANES):
        sh_ref[rho] = win[rho:rho + shifted_len]
    parts = []
    for cb in range(CONV_WIDTH // LANES):
        lanes = slice(cb * LANES, (cb + 1) * LANES)
        part = jnp.zeros((tr, LANES), F32) + b_ref[:, lanes]
        for k in range(CONV_K):
            rho = (k + lead) % SUBLANES
            a = (k + lead) - rho
            part = part + sh_ref[rho, a:a + tr, lanes] * w[k:k + 1, lanes]
        parts.append(part)
    acc = jnp.concatenate(parts, axis=1)
    mu = jnp.mean(acc, axis=-1, keepdims=True)
    xc = acc - mu
    y = xc * lax.rsqrt(jnp.mean(xc * xc, axis=-1, keepdims=True) + EPS) * lng_ref[...] + lnb_ref[...]
    y = y * jax.nn.sigmoid(y)
    z = y * lax.rsqrt(jnp.mean(y * y, axis=-1, keepdims=True) + EPS) * og_ref[...]
    o_ref[0] = z.astype(BF16)


def _conv(u, lw):
    b, s, c = u.shape
    tr = CONV_TILE
    halo_per_tile = tr // CONV_HALO
    n_halo = s // CONV_HALO
    fixed = lambda bi, i: (0, 0)
    full = lambda a: pl.BlockSpec(a.shape, fixed)
    return pl.pallas_call(
        _conv_kernel,
        grid=(b, s // tr),
        in_specs=[
            pl.BlockSpec((1, CONV_HALO, c), lambda bi, i: (bi, jnp.maximum(i * halo_per_tile - 1, 0), 0)),
            pl.BlockSpec((1, tr, c), lambda bi, i: (bi, i, 0)),
            pl.BlockSpec((1, CONV_HALO, c),
                         lambda bi, i: (bi, jnp.minimum((i + 1) * halo_per_tile, n_halo - 1), 0)),
            full(lw["conv_dw_w"]), full(lw["conv_dw_b"]), full(lw["conv_ln_g"]),
            full(lw["conv_ln_b"]), full(lw["conv_out_g"]),
        ],
        out_specs=pl.BlockSpec((1, tr, c), lambda bi, i: (bi, i, 0)),
        out_shape=jax.ShapeDtypeStruct((b, s, c), BF16),
        scratch_shapes=[pltpu.VMEM((SUBLANES, tr + 2 * CONV_HALO - SUBLANES, c), F32)],
        compiler_params=_params("arbitrary", "arbitrary"),
        name="conv",
    )(u, u, u, lw["conv_dw_w"], lw["conv_dw_b"], lw["conv_ln_g"], lw["conv_ln_b"], lw["conv_out_g"])


def _out_proj_kernel(x_ref, a_ref, c_ref, ag_ref, wa_ref, wc_ref, fg_ref, wrh_ref, wrl_ref,
                     x1_out, hf_out, aff_out):
    a = a_ref[...]
    an = (a * lax.rsqrt(jnp.mean(a * a, axis=-1, keepdims=True) + EPS) * ag_ref[...]).astype(BF16)
    x1 = (x_ref[...]
          + jnp.dot(an, wa_ref[...], preferred_element_type=F32)
          + jnp.dot(c_ref[...], wc_ref[...], preferred_element_type=F32))
    x1_out[...] = x1
    hf = x1 * lax.rsqrt(jnp.mean(x1 * x1, axis=-1, keepdims=True) + EPS) * fg_ref[...]
    hf_hi, hf_lo = _split_bf16(hf)
    hf_out[...] = hf_hi
    nt = (((1,), (1,)), ((), ()))
    logits = (lax.dot_general(wrh_ref[...], hf_hi, nt, preferred_element_type=F32)
              + lax.dot_general(wrh_ref[...], hf_lo, nt, preferred_element_type=F32)
              + lax.dot_general(wrl_ref[...], hf_hi, nt, preferred_element_type=F32))
    mx = jnp.max(logits, axis=0, keepdims=True)
    ex = jnp.exp(logits - mx)
    aff_out[...] = ex / jnp.sum(ex, axis=0, keepdims=True)


def _out_proj(x2d, attn2d, conv2d, lw):
    n = x2d.shape[0]
    tm = TOKEN_TILE
    row = lambda i: (i, 0)
    fixed = lambda i: (0, 0)
    full = lambda a: pl.BlockSpec(a.shape, fixed)
    return pl.pallas_call(
        _out_proj_kernel,
        grid=(n // tm,),
        in_specs=[
            pl.BlockSpec((tm, D_MODEL), row),
            pl.BlockSpec((tm, ATTN_WIDTH), row),
            pl.BlockSpec((tm, CONV_WIDTH), row),
            full(lw["attn_out_g"]), full(lw["w_out_a"]), full(lw["w_out_c"]), full(lw["ffn_norm_g"]),
            full(lw["wr_hi"]), full(lw["wr_lo"]),
        ],
        out_specs=[
            pl.BlockSpec((tm, D_MODEL), row),
            pl.BlockSpec((tm, D_MODEL), row),
            pl.BlockSpec((N_EXPERTS, tm), lambda i: (0, i)),
        ],
        out_shape=[
            jax.ShapeDtypeStruct((n, D_MODEL), F32),
            jax.ShapeDtypeStruct((n, D_MODEL), BF16),
            jax.ShapeDtypeStruct((N_EXPERTS, n), F32),
        ],
        compiler_params=_params("arbitrary"),
        name="out_proj",
    )(x2d, attn2d, conv2d, lw["attn_out_g"], lw["w_out_a"], lw["w_out_c"], lw["ffn_norm_g"],
      lw["wr_hi"], lw["wr_lo"])


def _select_kernel(aff_ref, slot_out, off_out, *, capacity):
    a = aff_ref[...]
    e, nc, _ = a.shape
    keys = lax.bitcast_convert_type(a, jnp.int32)

    def count(pred):
        c = jnp.sum(pred.astype(F32), axis=1, keepdims=True)
        return jnp.sum(c, axis=2, keepdims=True)

    cap = float(capacity)

    def bit_step(i, t):
        cand = t | jnp.left_shift(jnp.int32(1), 30 - i)
        return jnp.where(count(keys >= cand) >= cap, cand, t)

    thr = lax.fori_loop(0, 31, bit_step, jnp.zeros((e, 1, 1), jnp.int32))
    above = keys > thr
    tied = keys == thr
    need = cap - count(above)

    li = lax.broadcasted_iota(jnp.int32, (LANES, LANES), 0)
    lj = lax.broadcasted_iota(jnp.int32, (LANES, LANES), 1)
    before_lane = (li < lj).astype(BF16)
    ci = lax.broadcasted_iota(jnp.int32, (nc, nc), 0)
    cj = lax.broadcasted_iota(jnp.int32, (nc, nc), 1)
    before_chunk = jnp.broadcast_to((cj < ci).astype(BF16)[None], (e, nc, nc))

    def prefix(pred):
        mb = pred.astype(BF16)
        inchunk = jnp.dot(mb.reshape(e * nc, LANES), before_lane,
                          preferred_element_type=F32).reshape(e, nc, LANES)
        part = lax.dot_general(before_chunk, mb, (((2,), (1,)), ((0,), (0,))),
                               preferred_element_type=F32)
        off = jnp.sum(part, axis=2, keepdims=True)
        return inchunk + off, off

    tie_rank, _ = prefix(tied)
    mask = above | (tied & (tie_rank < need))
    rank, off = prefix(mask)
    slot_out[...] = jnp.where(mask, rank, -1.0).astype(jnp.int32)
    off_out[...] = jnp.broadcast_to(off, off_out.shape).astype(jnp.int32)


def _select(aff3, capacity):
    e, nc, _ = aff3.shape
    return pl.pallas_call(
        functools.partial(_select_kernel, capacity=capacity),
        out_shape=[
            jax.ShapeDtypeStruct((e, nc, LANES), jnp.int32),
            jax.ShapeDtypeStruct((e, nc, LANES), jnp.int32),
        ],
        compiler_params=pltpu.CompilerParams(vmem_limit_bytes=VMEM_LIMIT_BYTES),
        name="select",
    )(aff3)


def _gather_kernel(bnd_ref, slot_ref, h_ref, xg_ref):
    eb = pl.program_id(0)
    s = pl.program_id(1)

    @pl.when(s == 0)
    def _():
        xg_ref[...] = jnp.zeros_like(xg_ref)

    sub = lax.broadcasted_iota(jnp.int32, (GATHER_WINDOW, GATHER_CHUNK), 0)
    for ci in range(GATHER_SUB):
        h = h_ref[ci * GATHER_CHUNK:(ci + 1) * GATHER_CHUNK, :]
        for ee in range(GATHER_EXPERTS):
            e = eb * GATHER_EXPERTS + ee
            start = (bnd_ref[e, s * GATHER_SUB + ci] // BF16_ROWS) * BF16_ROWS
            srow = slot_ref[ee, ci:ci + 1, :]
            onehot = (srow - start == sub).astype(BF16)
            rows = jnp.dot(onehot, h, preferred_element_type=F32).astype(BF16)
            dst = pl.ds(pl.multiple_of(start, BF16_ROWS), GATHER_WINDOW)
            xg_ref[ee, dst, :] = xg_ref[ee, dst, :] + rows


def _gather(bounds, slot2, hf, capacity):
    n = hf.shape[0]
    chunks = n // GATHER_CHUNK
    rows = GATHER_SUB * GATHER_CHUNK
    padded = capacity + GATHER_WINDOW
    return pl.pallas_call(
        _gather_kernel,
        grid_spec=pltpu.PrefetchScalarGridSpec(
            num_scalar_prefetch=1,
            grid=(N_EXPERTS // GATHER_EXPERTS, chunks // GATHER_SUB),
            in_specs=[
                pl.BlockSpec((GATHER_EXPERTS, GATHER_SUB, GATHER_CHUNK), lambda e, s, b_: (e, s, 0)),
                pl.BlockSpec((rows, D_MODEL), lambda e, s, b_: (s, 0)),
            ],
            out_specs=pl.BlockSpec((GATHER_EXPERTS, padded, D_MODEL), lambda e, s, b_: (e, 0, 0)),
        ),
        out_shape=jax.ShapeDtypeStruct((N_EXPERTS, padded, D_MODEL), BF16),
        compiler_params=_params("arbitrary", "arbitrary"),
        name="gather",
    )(bounds, slot2.reshape(N_EXPERTS, chunks, GATHER_CHUNK), hf)


def _ffn_kernel(xg_ref, wg_ref, wu_ref, wd_ref, o_ref, acc_ref):
    f = pl.program_id(1)

    @pl.when(f == 0)
    def _():
        acc_ref[...] = jnp.zeros_like(acc_ref)

    wg = wg_ref[0, 0].astype(BF16)
    wu = wu_ref[0, 0].astype(BF16)
    wd = wd_ref[0, 0].astype(BF16)
    cap = acc_ref.shape[0]
    block = min(FFN_ROWS, cap)
    for r in range(cap // block):
        rows = pl.ds(r * block, block)
        x = xg_ref[0, rows, :]
        a = jnp.dot(x, wg, preferred_element_type=F32)
        b = jnp.dot(x, wu, preferred_element_type=F32)
        h = (a * jax.nn.sigmoid(a) * b).astype(BF16)
        acc_ref[rows, :] += jnp.dot(h, wd, preferred_element_type=F32)

    @pl.when(f == pl.num_programs(1) - 1)
    def _():
        o_ref[0, pl.ds(0, cap), :] = acc_ref[...].astype(BF16)
        o_ref[0, pl.ds(cap, CAP_PAD), :] = jnp.zeros((CAP_PAD, o_ref.shape[2]), BF16)


def _ffn(xg, w_gate, w_up, w_down, layer, capacity):
    e, _, d = xg.shape
    return pl.pallas_call(
        _ffn_kernel,
        grid=(e, EXPERT_FF // FF_CHUNK),
        in_specs=[
            pl.BlockSpec((1, capacity, d), lambda ei, f: (ei, 0, 0)),
            pl.BlockSpec((1, 1, d, FF_CHUNK), lambda ei, f: (layer, ei, 0, f)),
            pl.BlockSpec((1, 1, d, FF_CHUNK), lambda ei, f: (layer, ei, 0, f)),
            pl.BlockSpec((1, 1, FF_CHUNK, d), lambda ei, f: (layer, ei, f, 0)),
        ],
        out_specs=pl.BlockSpec((1, capacity + CAP_PAD, d), lambda ei, f: (ei, 0, 0)),
        out_shape=jax.ShapeDtypeStruct((e, capacity + CAP_PAD, d), BF16),
        scratch_shapes=[pltpu.VMEM((capacity, d), F32)],
        compiler_params=_params("arbitrary", "arbitrary"),
        name="ffn",
    )(xg, w_gate, w_up, w_down)


def _combine_kernel(lo_ref, x1_ref, slot_ref, aff_ref, *rest):
    w_refs, y_ref = rest[:N_EXPERTS], rest[N_EXPERTS]
    t = pl.program_id(0)
    lane = lax.broadcasted_iota(jnp.int32, (COMBINE_TILE, COMBINE_WINDOW), 1)
    slot = slot_ref[...]
    aff = aff_ref[...]
    onehots = []
    for e in range(N_EXPERTS):
        rel = slot[:, e:e + 1] - lo_ref[t, e]
        onehots.append((rel == lane).astype(BF16))
    for cb in range(D_MODEL // COMBINE_COLS):
        cols = pl.ds(cb * COMBINE_COLS, COMBINE_COLS)
        y = x1_ref[:, cols]
        for e in range(N_EXPERTS):
            y = y + aff[:, e:e + 1] * jnp.dot(onehots[e], w_refs[e][0, :, cols],
                                              preferred_element_type=F32)
        y_ref[:, cols] = y


def _combine(lo_aligned, x1, slot_t, aff_t, out):
    n = x1.shape[0]
    tok = lambda t, lo: (t, 0)

    def window(e):
        return pl.BlockSpec(
            (pl.Element(1), pl.Element(COMBINE_WINDOW), pl.Element(D_MODEL)),
            lambda t, lo: (e, pl.multiple_of(lo[t, e], BF16_ROWS), 0))

    return pl.pallas_call(
        _combine_kernel,
        grid_spec=pltpu.PrefetchScalarGridSpec(
            num_scalar_prefetch=1,
            grid=(n // COMBINE_TILE,),
            in_specs=[
                pl.BlockSpec((COMBINE_TILE, D_MODEL), tok),
                pl.BlockSpec((COMBINE_TILE, N_EXPERTS), tok),
                pl.BlockSpec((COMBINE_TILE, N_EXPERTS), tok),
            ] + [window(e) for e in range(N_EXPERTS)],
            out_specs=pl.BlockSpec((COMBINE_TILE, D_MODEL), tok),
        ),
        out_shape=jax.ShapeDtypeStruct((n, D_MODEL), F32),
        compiler_params=_params("arbitrary"),
        name="combine",
    )(lo_aligned, x1, slot_t, aff_t, *([out] * N_EXPERTS))


def _head_perms():
    ev = np.arange(0, HEAD_DIM, 2)
    od = np.arange(1, HEAD_DIM, 2)
    return np.concatenate([ev, od]), np.concatenate([od, ev])


def _rope_tables(seq):
    rows = seq // GRID_W
    row = jnp.repeat(jnp.arange(rows, dtype=F32), GRID_W)
    col = jnp.tile(jnp.arange(GRID_W, dtype=F32), rows)
    inv_freq = ROPE_THETA ** (-jnp.arange(0, AXIS_DIM, 2, dtype=F32) / AXIS_DIM)
    ang = jnp.concatenate([row[:, None] * inv_freq, col[:, None] * inv_freq], axis=-1)
    cos, sin = jnp.cos(ang), jnp.sin(ang)
    c_head = jnp.concatenate([cos, cos], axis=-1)
    s_head = jnp.concatenate([-sin, sin], axis=-1)
    scale = HEAD_DIM ** -0.5
    ones_block = jnp.ones((HEAD_DIM, HEAD_DIM), F32)
    return {
        "cq": jnp.tile(c_head, (1, N_HEADS)) * scale,
        "sq": jnp.tile(s_head, (1, N_HEADS)) * scale,
        "ck": jnp.tile(c_head, (1, 2 * N_KV_HEADS)),
        "sk": jnp.tile(s_head, (1, 2 * N_KV_HEADS)),
        "bq": jnp.kron(jnp.eye(N_HEADS, dtype=F32), ones_block).astype(BF16),
        "bk": jnp.kron(jnp.eye(2 * N_KV_HEADS, dtype=F32), ones_block).astype(BF16),
    }


def _layer_weights(l, p):
    perm, perm_sw = _head_perms()
    w_in = p["w_in"][l]
    o_k = ATTN_WIDTH
    o_v = o_k + KV_WIDTH
    o_cv = o_v + KV_WIDTH
    o_cg = o_cv + CONV_WIDTH
    q_cols = np.concatenate([h * HEAD_DIM + perm for h in range(N_HEADS)])
    qs_cols = np.concatenate([h * HEAD_DIM + perm_sw for h in range(N_HEADS)])
    kv_heads = [g for g in range(N_KV_HEADS) for _ in range(2)]
    k_cols = np.concatenate([o_k + g * HEAD_DIM + perm for g in kv_heads])
    ks_cols = np.concatenate([o_k + g * HEAD_DIM + perm_sw for g in kv_heads])
    v_cols = np.concatenate([o_v + g * HEAD_DIM + np.arange(HEAD_DIM) for g in kv_heads])
    cols = np.concatenate([q_cols, qs_cols, k_cols, ks_cols, v_cols,
                           np.arange(o_cv, o_cg), np.arange(o_cg, o_cg + CONV_WIDTH)])
    assert cols.shape[0] == IN_COLS
    row2 = lambda a: a.reshape(1, -1)
    w_router_t = p["w_router"][l].T
    wr_hi, wr_lo = _split_bf16(w_router_t)
    w_out = p["w_out"][l].astype(BF16)
    return {
        "attn_norm_g": row2(p["attn_norm_g"][l]),
        "w_in": w_in[:, cols].astype(BF16),
        "gq": row2(jnp.tile(p["q_norm_g"][l][perm], N_HEADS)),
        "gqs": row2(jnp.tile(p["q_norm_g"][l][perm_sw], N_HEADS)),
        "gk": row2(jnp.tile(p["k_norm_g"][l][perm], 2 * N_KV_HEADS)),
        "gks": row2(jnp.tile(p["k_norm_g"][l][perm_sw], 2 * N_KV_HEADS)),
        "conv_dw_w": p["conv_dw_w"][l],
        "conv_dw_b": row2(p["conv_dw_b"][l]),
        "conv_ln_g": row2(p["conv_ln_g"][l]),
        "conv_ln_b": row2(p["conv_ln_b"][l]),
        "conv_out_g": row2(p["conv_out_g"][l]),
        "attn_out_g": row2(p["attn_out_g"][l]),
        "w_out_a": w_out[:ATTN_WIDTH],
        "w_out_c": w_out[ATTN_WIDTH:],
        "ffn_norm_g": row2(p["ffn_norm_g"][l]),
        "wr_hi": wr_hi,
        "wr_lo": wr_lo,
        "layer": l,
        "w_gate": p["w_gate"],
        "w_up": p["w_up"],
        "w_down": p["w_down"],
    }


def _moe(x1, hf, aff, lw):
    n = x1.shape[0]
    capacity = CAPACITY_FACTOR * n // N_EXPERTS
    nc = n // LANES
    slot3, off3 = _select(aff.reshape(N_EXPERTS, nc, LANES), capacity)
    bounds = jnp.concatenate([off3[:, :, 0], jnp.full((N_EXPERTS, 1), capacity, jnp.int32)], axis=1)
    slot2 = slot3.reshape(N_EXPERTS, n)
    xg = _gather(bounds[:, ::GATHER_CHUNK // LANES], slot2, hf, capacity)
    out = _ffn(xg, lw["w_gate"], lw["w_up"], lw["w_down"], lw["layer"], capacity)
    lo = bounds[:, :nc].T
    return _combine((lo // BF16_ROWS) * BF16_ROWS, x1, slot2.T, aff.T, out)


def _layer(x2d, batch, seq, lw, tabs):
    q, k, v, u = _in_proj(x2d, seq, lw, tabs)
    attn = _attention(q.reshape(batch, seq, ATTN_WIDTH), k.reshape(batch, seq, KV_DUP_WIDTH),
                      v.reshape(batch, seq, KV_DUP_WIDTH))
    conv = _conv(u.reshape(batch, seq, CONV_WIDTH), lw)
    x1, hf, aff = _out_proj(x2d, attn.reshape(-1, ATTN_WIDTH), conv.reshape(-1, CONV_WIDTH), lw)
    return _moe(x1, hf, aff, lw)


def _trunk(x, layers):
    batch, seq, d = x.shape
    tabs = _rope_tables(seq)
    x2d = x.reshape(batch * seq, d)
    for lw in layers:
        x2d = _layer(x2d, batch, seq, lw, tabs)
    return x2d.reshape(batch, seq, d)


def kernel(x_prompt, x_sample, attn_norm_g, w_in, q_norm_g, k_norm_g, conv_dw_w, conv_dw_b, conv_ln_g, conv_ln_b, attn_out_g, conv_out_g, w_out, ffn_norm_g, w_router, w_gate, w_up, w_down):
    p = dict(attn_norm_g=attn_norm_g, w_in=w_in, q_norm_g=q_norm_g, k_norm_g=k_norm_g,
             conv_dw_w=conv_dw_w, conv_dw_b=conv_dw_b, conv_ln_g=conv_ln_g, conv_ln_b=conv_ln_b,
             attn_out_g=attn_out_g, conv_out_g=conv_out_g, w_out=w_out, ffn_norm_g=ffn_norm_g,
             w_router=w_router, w_gate=w_gate, w_up=w_up, w_down=w_down)
    layers = [_layer_weights(l, p) for l in range(w_in.shape[0])]
    return (_trunk(x_prompt, layers), _trunk(x_sample, layers))
```

```python
import functools

import numpy as np
import jax
import jax.numpy as jnp
from jax import lax
from jax.experimental import pallas as pl
from jax.experimental.pallas import tpu as pltpu

F32 = jnp.float32
BF16 = jnp.bfloat16

D_MODEL = 1024
HEAD_DIM = 64
N_HEADS = 8
N_KV_HEADS = 2
ATTN_WIDTH = N_HEADS * HEAD_DIM
KV_WIDTH = N_KV_HEADS * HEAD_DIM
KV_DUP_WIDTH = 2 * KV_WIDTH
CONV_WIDTH = D_MODEL - ATTN_WIDTH
CONV_K = 31
GRID_W = 64
ROPE_THETA = 10000.0
AXIS_DIM = HEAD_DIM // 2
N_EXPERTS = 16
EXPERT_FF = 2 * D_MODEL
CAPACITY_FACTOR = 2
EPS = 1e-6

LANES = 128
SUBLANES = 8
BF16_ROWS = 16
VMEM_LIMIT_BYTES = 56 * 1024 * 1024

TOKEN_TILE = 512
Q_TILE = 512
KV_CHUNK = 512
CONV_TILE = 128
CONV_HALO = 16
GATHER_CHUNK = 256
GATHER_SUB = 8
GATHER_EXPERTS = 2
GATHER_WINDOW = GATHER_CHUNK + BF16_ROWS
FF_CHUNK = 512
FFN_ROWS = 512
COMBINE_TILE = LANES
COMBINE_WINDOW = COMBINE_TILE + BF16_ROWS
COMBINE_COLS = 256
COMBINE_STEP_TILES = 2
CAP_PAD = COMBINE_WINDOW


def _params(*semantics):
    return pltpu.CompilerParams(dimension_semantics=semantics, vmem_limit_bytes=VMEM_LIMIT_BYTES)


def _split_bf16(a):
    hi = a.astype(BF16)
    lo = (a - hi.astype(F32)).astype(BF16)
    return hi, lo


W_Q = (0, 512)
W_QS = (512, 1024)
W_K = (1024, 1280)
W_KS = (1280, 1536)
W_V = (1536, 1792)
W_CV = (1792, 2304)
W_CG = (2304, 2816)
IN_COLS = 2816


def _in_proj_kernel(x_ref, g_ref, w_ref, bq_ref, bk_ref, gq_ref, gqs_ref, gk_ref, gks_ref,
                    cq_ref, sq_ref, ck_ref, sk_ref, q_out, k_out, v_out, u_out):
    x = x_ref[...]
    ms = jnp.mean(x * x, axis=-1, keepdims=True)
    h = (x * lax.rsqrt(ms + EPS) * g_ref[...]).astype(BF16)
    proj = jnp.dot(h, w_ref[...], preferred_element_type=F32)

    def head_rsqrt(a, ones_blockdiag):
        hi, lo = _split_bf16(a * a)
        ss = (jnp.dot(hi, ones_blockdiag, preferred_element_type=F32)
              + jnp.dot(lo, ones_blockdiag, preferred_element_type=F32))
        return lax.rsqrt(ss * (1.0 / HEAD_DIM) + EPS)

    q = proj[:, W_Q[0]:W_Q[1]]
    qs = proj[:, W_QS[0]:W_QS[1]]
    rq = head_rsqrt(q, bq_ref[...])
    q_out[...] = ((q * rq * gq_ref[...]) * cq_ref[...]
                  + (qs * rq * gqs_ref[...]) * sq_ref[...]).astype(BF16)

    k = proj[:, W_K[0]:W_K[1]]
    ks = proj[:, W_KS[0]:W_KS[1]]
    rk = head_rsqrt(k, bk_ref[...])
    k_out[...] = ((k * rk * gk_ref[...]) * ck_ref[...]
                  + (ks * rk * gks_ref[...]) * sk_ref[...]).astype(BF16)

    v_out[...] = proj[:, W_V[0]:W_V[1]].astype(BF16)
    cv = proj[:, W_CV[0]:W_CV[1]]
    cg = proj[:, W_CG[0]:W_CG[1]]
    u_out[...] = cv * jax.nn.sigmoid(cg)


def _in_proj(x2d, seq, lw, tabs):
    n = x2d.shape[0]
    tm = TOKEN_TILE
    seq_tiles = seq // tm
    row = lambda i: (i, 0)
    fixed = lambda i: (0, 0)
    pos = lambda i: (i % seq_tiles, 0)
    full = lambda a: pl.BlockSpec(a.shape, fixed)
    return pl.pallas_call(
        _in_proj_kernel,
        grid=(n // tm,),
        in_specs=[
            pl.BlockSpec((tm, D_MODEL), row),
            full(lw["attn_norm_g"]), full(lw["w_in"]), full(tabs["bq"]), full(tabs["bk"]),
            full(lw["gq"]), full(lw["gqs"]), full(lw["gk"]), full(lw["gks"]),
            pl.BlockSpec((tm, ATTN_WIDTH), pos), pl.BlockSpec((tm, ATTN_WIDTH), pos),
            pl.BlockSpec((tm, KV_DUP_WIDTH), pos), pl.BlockSpec((tm, KV_DUP_WIDTH), pos),
        ],
        out_specs=[
            pl.BlockSpec((tm, ATTN_WIDTH), row),
            pl.BlockSpec((tm, KV_DUP_WIDTH), row),
            pl.BlockSpec((tm, KV_DUP_WIDTH), row),
            pl.BlockSpec((tm, CONV_WIDTH), row),
        ],
        out_shape=[
            jax.ShapeDtypeStruct((n, ATTN_WIDTH), BF16),
            jax.ShapeDtypeStruct((n, KV_DUP_WIDTH), BF16),
            jax.ShapeDtypeStruct((n, KV_DUP_WIDTH), BF16),
            jax.ShapeDtypeStruct((n, CONV_WIDTH), F32),
        ],
        compiler_params=_params("arbitrary"),
        name="in_proj",
    )(x2d, lw["attn_norm_g"], lw["w_in"], tabs["bq"], tabs["bk"],
      lw["gq"], lw["gqs"], lw["gk"], lw["gks"],
      tabs["cq"], tabs["sq"], tabs["ck"], tabs["sk"])


def _attention_kernel(q_ref, k_ref, v_ref, o_ref):
    q = q_ref[0]
    tq = q.shape[0]
    lane = lax.broadcasted_iota(jnp.int32, q.shape, 1)
    first = lane < HEAD_DIM
    zero = jnp.zeros_like(q)
    q2 = jnp.concatenate([jnp.where(first, q, zero), jnp.where(first, zero, q)], axis=0)
    seq = k_ref.shape[1]
    vlane = lax.broadcasted_iota(jnp.int32, (KV_CHUNK, LANES), 1) < HEAD_DIM
    one = jnp.ones((KV_CHUNK, LANES), BF16)
    m = jnp.full((2 * tq, 1), -jnp.inf, F32)
    acc_a = jnp.zeros((tq, LANES), F32)
    acc_b = jnp.zeros((tq, LANES), F32)
    for c in range(seq // KV_CHUNK):
        keys = pl.ds(c * KV_CHUNK, KV_CHUNK)
        s = lax.dot_general(q2, k_ref[0, keys, :], (((1,), (1,)), ((), ())), preferred_element_type=F32)
        m_new = jnp.maximum(m, jnp.max(s, axis=-1, keepdims=True))
        alpha = jnp.exp2(m - m_new)
        p = jnp.exp2(s - m_new).astype(BF16)
        v = v_ref[0, keys, :]
        acc_a = alpha[:tq] * acc_a + jnp.dot(p[:tq], jnp.where(vlane, v, one), preferred_element_type=F32)
        acc_b = alpha[tq:] * acc_b + jnp.dot(p[tq:], jnp.where(vlane, one, v), preferred_element_type=F32)
        m = m_new
    o_a = acc_a / pltpu.roll(acc_a, HEAD_DIM, 1)
    o_b = acc_b / pltpu.roll(acc_b, HEAD_DIM, 1)
    o_ref[0] = jnp.where(first, o_a, o_b)


def _attention(q, k, v):
    b, s, _ = q.shape
    pairs = N_HEADS // 2
    pairs_per_kv = pairs // N_KV_HEADS
    return pl.pallas_call(
        _attention_kernel,
        grid=(b, pairs, s // Q_TILE),
        in_specs=[
            pl.BlockSpec((1, Q_TILE, LANES), lambda bi, j, i: (bi, i, j)),
            pl.BlockSpec((1, s, LANES), lambda bi, j, i: (bi, 0, j // pairs_per_kv)),
            pl.BlockSpec((1, s, LANES), lambda bi, j, i: (bi, 0, j // pairs_per_kv)),
        ],
        out_specs=pl.BlockSpec((1, Q_TILE, LANES), lambda bi, j, i: (bi, i, j)),
        out_shape=jax.ShapeDtypeStruct((b, s, ATTN_WIDTH), F32),
        compiler_params=_params("arbitrary", "arbitrary", "arbitrary"),
        name="attention",
    )(q, k, v)


def _conv_kernel(up_ref, uc_ref, un_ref, w_ref, b_ref, lng_ref, lnb_ref, og_ref, o_ref, sh_ref):
    i = pl.program_id(1)
    last = pl.num_programs(1) - 1
    prev = jnp.where(i > 0, up_ref[0], 0.0)
    nxt = jnp.where(i < last, un_ref[0], 0.0)
    win = jnp.concatenate([prev, uc_ref[0], nxt], axis=0)
    tr = uc_ref.shape[1]
    w = w_ref[...]
    lead = CONV_HALO - CONV_K // 2
    shifted_len = tr + 2 * CONV_HALO - SUBLANES
    for rho in range(SUBLANES):
        sh_ref[rho] = win[rho:rho + shifted_len]
    parts = []
    for cb in range(CONV_WIDTH // LANES):
        lanes = slice(cb * LANES, (cb + 1) * LANES)
        part = jnp.zeros((tr, LANES), F32) + b_ref[:, lanes]
        for k in range(CONV_K):
            rho = (k + lead) % SUBLANES
            a = (k + lead) - rho
            part = part + sh_ref[rho, a:a + tr, lanes] * w[k:k + 1, lanes]
        parts.append(part)
    acc = jnp.concatenate(parts, axis=1)
    mu = jnp.mean(acc, axis=-1, keepdims=True)
    xc = acc - mu
    y = xc * lax.rsqrt(jnp.mean(xc * xc, axis=-1, keepdims=True) + EPS) * lng_ref[...] + lnb_ref[...]
    y = y * jax.nn.sigmoid(y)
    z = y * lax.rsqrt(jnp.mean(y * y, axis=-1, keepdims=True) + EPS) * og_ref[...]
    o_ref[0] = z.astype(BF16)


def _conv(u, lw):
    b, s, c = u.shape
    tr = CONV_TILE
    halo_per_tile = tr // CONV_HALO
    n_halo = s // CONV_HALO
    fixed = lambda bi, i: (0, 0)
    full = lambda a: pl.BlockSpec(a.shape, fixed)
    return pl.pallas_call(
        _conv_kernel,
        grid=(b, s // tr),
        in_specs=[
            pl.BlockSpec((1, CONV_HALO, c), lambda bi, i: (bi, jnp.maximum(i * halo_per_tile - 1, 0), 0)),
            pl.BlockSpec((1, tr, c), lambda bi, i: (bi, i, 0)),
            pl.BlockSpec((1, CONV_HALO, c),
                         lambda bi, i: (bi, jnp.minimum((i + 1) * halo_per_tile, n_halo - 1), 0)),
            full(lw["conv_dw_w"]), full(lw["conv_dw_b"]), full(lw["conv_ln_g"]),
            full(lw["conv_ln_b"]), full(lw["conv_out_g"]),
        ],
        out_specs=pl.BlockSpec((1, tr, c), lambda bi, i: (bi, i, 0)),
        out_shape=jax.ShapeDtypeStruct((b, s, c), BF16),
        scratch_shapes=[pltpu.VMEM((SUBLANES, tr + 2 * CONV_HALO - SUBLANES, c), F32)],
        compiler_params=_params("arbitrary", "arbitrary"),
        name="conv",
    )(u, u, u, lw["conv_dw_w"], lw["conv_dw_b"], lw["conv_ln_g"], lw["conv_ln_b"], lw["conv_out_g"])


def _out_proj_kernel(x_ref, a_ref, c_ref, ag_ref, wa_ref, wc_ref, fg_ref, wrh_ref, wrl_ref,
                     x1_out, hf_out, aff_out):
    a = a_ref[...]
    an = (a * lax.rsqrt(jnp.mean(a * a, axis=-1, keepdims=True) + EPS) * ag_ref[...]).astype(BF16)
    x1 = (x_ref[...]
          + jnp.dot(an, wa_ref[...], preferred_element_type=F32)
          + jnp.dot(c_ref[...], wc_ref[...], preferred_element_type=F32))
    x1_out[...] = x1
    hf = x1 * lax.rsqrt(jnp.mean(x1 * x1, axis=-1, keepdims=True) + EPS) * fg_ref[...]
    hf_hi, hf_lo = _split_bf16(hf)
    hf_out[...] = hf_hi
    nt = (((1,), (1,)), ((), ()))
    logits = (lax.dot_general(wrh_ref[...], hf_hi, nt, preferred_element_type=F32)
              + lax.dot_general(wrh_ref[...], hf_lo, nt, preferred_element_type=F32)
              + lax.dot_general(wrl_ref[...], hf_hi, nt, preferred_element_type=F32))
    mx = jnp.max(logits, axis=0, keepdims=True)
    ex = jnp.exp(logits - mx)
    aff_out[...] = ex / jnp.sum(ex, axis=0, keepdims=True)


def _out_proj(x2d, attn2d, conv2d, lw):
    n = x2d.shape[0]
    tm = TOKEN_TILE
    row = lambda i: (i, 0)
    fixed = lambda i: (0, 0)
    full = lambda a: pl.BlockSpec(a.shape, fixed)
    return pl.pallas_call(
        _out_proj_kernel,
        grid=(n // tm,),
        in_specs=[
            pl.BlockSpec((tm, D_MODEL), row),
            pl.BlockSpec((tm, ATTN_WIDTH), row),
            pl.BlockSpec((tm, CONV_WIDTH), row),
            full(lw["attn_out_g"]), full(lw["w_out_a"]), full(lw["w_out_c"]), full(lw["ffn_norm_g"]),
            full(lw["wr_hi"]), full(lw["wr_lo"]),
        ],
        out_specs=[
            pl.BlockSpec((tm, D_MODEL), row),
            pl.BlockSpec((tm, D_MODEL), row),
            pl.BlockSpec((N_EXPERTS, tm), lambda i: (0, i)),
        ],
        out_shape=[
            jax.ShapeDtypeStruct((n, D_MODEL), F32),
            jax.ShapeDtypeStruct((n, D_MODEL), BF16),
            jax.ShapeDtypeStruct((N_EXPERTS, n), F32),
        ],
        compiler_params=_params("arbitrary"),
        name="out_proj",
    )(x2d, attn2d, conv2d, lw["attn_out_g"], lw["w_out_a"], lw["w_out_c"], lw["ffn_norm_g"],
      lw["wr_hi"], lw["wr_lo"])


def _select_kernel(aff_ref, slot_out, off_out, *, capacity):
    a = aff_ref[...]
    e, nc, _ = a.shape
    keys = lax.bitcast_convert_type(a, jnp.int32)

    def count(pred):
        c = jnp.sum(pred.astype(F32), axis=1, keepdims=True)
        return jnp.sum(c, axis=2, keepdims=True)

    cap = float(capacity)

    def bit_step(i, t):
        cand = t | jnp.left_shift(jnp.int32(1), 30 - i)
        return jnp.where(count(keys >= cand) >= cap, cand, t)

    thr = lax.fori_loop(0, 31, bit_step, jnp.zeros((e, 1, 1), jnp.int32))
    above = keys > thr
    tied = keys == thr
    need = cap - count(above)

    li = lax.broadcasted_iota(jnp.int32, (LANES, LANES), 0)
    lj = lax.broadcasted_iota(jnp.int32, (LANES, LANES), 1)
    before_lane = (li < lj).astype(BF16)
    ci = lax.broadcasted_iota(jnp.int32, (nc, nc), 0)
    cj = lax.broadcasted_iota(jnp.int32, (nc, nc), 1)
    before_chunk = jnp.broadcast_to((cj < ci).astype(BF16)[None], (e, nc, nc))

    def prefix(pred):
        mb = pred.astype(BF16)
        inchunk = jnp.dot(mb.reshape(e * nc, LANES), before_lane,
                          preferred_element_type=F32).reshape(e, nc, LANES)
        part = lax.dot_general(before_chunk, mb, (((2,), (1,)), ((0,), (0,))),
                               preferred_element_type=F32)
        off = jnp.sum(part, axis=2, keepdims=True)
        return inchunk + off, off

    tie_rank, _ = prefix(tied)
    mask = above | (tied & (tie_rank < need))
    rank, off = prefix(mask)
    slot_out[...] = jnp.where(mask, rank, -1.0).astype(jnp.int32)
    off_out[...] = jnp.broadcast_to(off, off_out.shape).astype(jnp.int32)


def _select(aff3, capacity):
    e, nc, _ = aff3.shape
    return pl.pallas_call(
        functools.partial(_select_kernel, capacity=capacity),
        out_shape=[
            jax.ShapeDtypeStruct((e, nc, LANES), jnp.int32),
            jax.ShapeDtypeStruct((e, nc, LANES), jnp.int32),
        ],
        compiler_params=pltpu.CompilerParams(vmem_limit_bytes=VMEM_LIMIT_BYTES),
        name="select",
    )(aff3)


def _gather_kernel(bnd_ref, slot_ref, aff_ref, h_ref, xg_ref, gate_ref):
    eb = pl.program_id(0)
    s = pl.program_id(1)

    @pl.when(s == 0)
    def _():
        xg_ref[...] = jnp.zeros_like(xg_ref)
        gate_ref[...] = jnp.zeros_like(gate_ref)

    sub = lax.broadcasted_iota(jnp.int32, (GATHER_WINDOW, GATHER_CHUNK), 0)
    for ci in range(GATHER_SUB):
        h = h_ref[ci * GATHER_CHUNK:(ci + 1) * GATHER_CHUNK, :]
        for ee in range(GATHER_EXPERTS):
            e = eb * GATHER_EXPERTS + ee
            start = (bnd_ref[e, s * GATHER_SUB + ci] // BF16_ROWS) * BF16_ROWS
            srow = slot_ref[ee, ci:ci + 1, :]
            match = srow - start == sub
            rows = jnp.dot(match.astype(BF16), h, preferred_element_type=F32).astype(BF16)
            dst = pl.ds(pl.multiple_of(start, BF16_ROWS), GATHER_WINDOW)
            xg_ref[ee, dst, :] = xg_ref[ee, dst, :] + rows
            gate = jnp.sum(jnp.where(match, aff_ref[ee, ci:ci + 1, :], 0.0), axis=1, keepdims=True)
            gate_ref[ee, dst, :] = gate_ref[ee, dst, :] + gate


def _gather(bounds, slot2, aff, hf, capacity):
    n = hf.shape[0]
    chunks = n // GATHER_CHUNK
    rows = GATHER_SUB * GATHER_CHUNK
    padded = capacity + GATHER_WINDOW
    return pl.pallas_call(
        _gather_kernel,
        grid_spec=pltpu.PrefetchScalarGridSpec(
            num_scalar_prefetch=1,
            grid=(N_EXPERTS // GATHER_EXPERTS, chunks // GATHER_SUB),
            in_specs=[
                pl.BlockSpec((GATHER_EXPERTS, GATHER_SUB, GATHER_CHUNK), lambda e, s, b_: (e, s, 0)),
                pl.BlockSpec((GATHER_EXPERTS, GATHER_SUB, GATHER_CHUNK), lambda e, s, b_: (e, s, 0)),
                pl.BlockSpec((rows, D_MODEL), lambda e, s, b_: (s, 0)),
            ],
            out_specs=[
                pl.BlockSpec((GATHER_EXPERTS, padded, D_MODEL), lambda e, s, b_: (e, 0, 0)),
                pl.BlockSpec((GATHER_EXPERTS, padded, 1), lambda e, s, b_: (e, 0, 0)),
            ],
        ),
        out_shape=[
            jax.ShapeDtypeStruct((N_EXPERTS, padded, D_MODEL), BF16),
            jax.ShapeDtypeStruct((N_EXPERTS, padded, 1), F32),
        ],
        compiler_params=_params("arbitrary", "arbitrary"),
        name="gather",
    )(bounds, slot2.reshape(N_EXPERTS, chunks, GATHER_CHUNK), aff.reshape(N_EXPERTS, chunks, GATHER_CHUNK), hf)


def _ffn_kernel(xg_ref, gate_ref, wg_ref, wu_ref, wd_ref, o_ref, acc_ref):
    f = pl.program_id(1)

    @pl.when(f == 0)
    def _():
        acc_ref[...] = jnp.zeros_like(acc_ref)

    wg = wg_ref[0, 0].astype(BF16)
    wu = wu_ref[0, 0].astype(BF16)
    wd = wd_ref[0, 0].astype(BF16)
    cap = acc_ref.shape[0]
    block = min(FFN_ROWS, cap)
    for r in range(cap // block):
        rows = pl.ds(r * block, block)
        x = xg_ref[0, rows, :]
        a = jnp.dot(x, wg, preferred_element_type=F32)
        b = jnp.dot(x, wu, preferred_element_type=F32)
        h = (a * jax.nn.sigmoid(a) * b).astype(BF16)
        acc_ref[rows, :] += jnp.dot(h, wd, preferred_element_type=F32)

    @pl.when(f == pl.num_programs(1) - 1)
    def _():
        o_ref[0, pl.ds(0, cap), :] = (acc_ref[...] * gate_ref[0, pl.ds(0, cap), :]).astype(BF16)
        o_ref[0, pl.ds(cap, CAP_PAD), :] = jnp.zeros((CAP_PAD, o_ref.shape[2]), BF16)


def _ffn(xg, gate, w_gate, w_up, w_down, layer, capacity):
    e, _, d = xg.shape
    return pl.pallas_call(
        _ffn_kernel,
        grid=(e, EXPERT_FF // FF_CHUNK),
        in_specs=[
            pl.BlockSpec((1, capacity, d), lambda ei, f: (ei, 0, 0)),
            pl.BlockSpec((1, capacity, 1), lambda ei, f: (ei, 0, 0)),
            pl.BlockSpec((1, 1, d, FF_CHUNK), lambda ei, f: (layer, ei, 0, f)),
            pl.BlockSpec((1, 1, d, FF_CHUNK), lambda ei, f: (layer, ei, 0, f)),
            pl.BlockSpec((1, 1, FF_CHUNK, d), lambda ei, f: (layer, ei, f, 0)),
        ],
        out_specs=pl.BlockSpec((1, capacity + CAP_PAD, d), lambda ei, f: (ei, 0, 0)),
        out_shape=jax.ShapeDtypeStruct((e, capacity + CAP_PAD, d), BF16),
        scratch_shapes=[pltpu.VMEM((capacity, d), F32)],
        compiler_params=_params("arbitrary", "arbitrary"),
        name="ffn",
    )(xg, gate, w_gate, w_up, w_down)


def _combine_kernel(lo_ref, x1_ref, slot_ref, *rest):
    n_win = COMBINE_STEP_TILES * N_EXPERTS
    w_refs, y_ref = rest[:n_win], rest[n_win]
    step = pl.program_id(0)
    lane = lax.broadcasted_iota(jnp.int32, (COMBINE_TILE, COMBINE_WINDOW), 1)
    for tt in range(COMBINE_STEP_TILES):
        t = step * COMBINE_STEP_TILES + tt
        rows = pl.ds(tt * COMBINE_TILE, COMBINE_TILE)
        slot = slot_ref[rows, :]
        onehots = []
        for e in range(N_EXPERTS):
            rel = slot[:, e:e + 1] - lo_ref[t, e]
            onehots.append((rel == lane).astype(BF16))
        for cb in range(D_MODEL // COMBINE_COLS):
            cols = pl.ds(cb * COMBINE_COLS, COMBINE_COLS)
            y = x1_ref[rows, cols]
            for e in range(N_EXPERTS):
                y = y + jnp.dot(onehots[e], w_refs[tt * N_EXPERTS + e][0, :, cols],
                                preferred_element_type=F32)
            y_ref[rows, cols] = y


def _combine(lo_aligned, x1, slot_t, out):
    n = x1.shape[0]
    step_rows = COMBINE_STEP_TILES * COMBINE_TILE
    tok = lambda s, lo: (s, 0)

    def window(tt, e):
        return pl.BlockSpec(
            (pl.Element(1), pl.Element(COMBINE_WINDOW), pl.Element(D_MODEL)),
            lambda s, lo: (e, pl.multiple_of(lo[s * COMBINE_STEP_TILES + tt, e], BF16_ROWS), 0))

    windows = [window(tt, e) for tt in range(COMBINE_STEP_TILES) for e in range(N_EXPERTS)]
    return pl.pallas_call(
        _combine_kernel,
        grid_spec=pltpu.PrefetchScalarGridSpec(
            num_scalar_prefetch=1,
            grid=(n // step_rows,),
            in_specs=[
                pl.BlockSpec((step_rows, D_MODEL), tok),
                pl.BlockSpec((step_rows, N_EXPERTS), tok),
            ] + windows,
            out_specs=pl.BlockSpec((step_rows, D_MODEL), tok),
        ),
        out_shape=jax.ShapeDtypeStruct((n, D_MODEL), F32),
        compiler_params=_params("arbitrary"),
        name="combine",
    )(lo_aligned, x1, slot_t, *([out] * len(windows)))


def _head_perms():
    ev = np.arange(0, HEAD_DIM, 2)
    od = np.arange(1, HEAD_DIM, 2)
    return np.concatenate([ev, od]), np.concatenate([od, ev])


def _rope_tables(seq):
    rows = seq // GRID_W
    row = jnp.repeat(jnp.arange(rows, dtype=F32), GRID_W)
    col = jnp.tile(jnp.arange(GRID_W, dtype=F32), rows)
    inv_freq = ROPE_THETA ** (-jnp.arange(0, AXIS_DIM, 2, dtype=F32) / AXIS_DIM)
    ang = jnp.concatenate([row[:, None] * inv_freq, col[:, None] * inv_freq], axis=-1)
    cos, sin = jnp.cos(ang), jnp.sin(ang)
    c_head = jnp.concatenate([cos, cos], axis=-1)
    s_head = jnp.concatenate([-sin, sin], axis=-1)
    scale = HEAD_DIM ** -0.5 * float(np.log2(np.e))
    ones_block = jnp.ones((HEAD_DIM, HEAD_DIM), F32)
    return {
        "cq": jnp.tile(c_head, (1, N_HEADS)) * scale,
        "sq": jnp.tile(s_head, (1, N_HEADS)) * scale,
        "ck": jnp.tile(c_head, (1, 2 * N_KV_HEADS)),
        "sk": jnp.tile(s_head, (1, 2 * N_KV_HEADS)),
        "bq": jnp.kron(jnp.eye(N_HEADS, dtype=F32), ones_block).astype(BF16),
        "bk": jnp.kron(jnp.eye(2 * N_KV_HEADS, dtype=F32), ones_block).astype(BF16),
    }


def _layer_weights(l, p):
    perm, perm_sw = _head_perms()
    w_in = p["w_in"][l]
    o_k = ATTN_WIDTH
    o_v = o_k + KV_WIDTH
    o_cv = o_v + KV_WIDTH
    o_cg = o_cv + CONV_WIDTH
    q_cols = np.concatenate([h * HEAD_DIM + perm for h in range(N_HEADS)])
    qs_cols = np.concatenate([h * HEAD_DIM + perm_sw for h in range(N_HEADS)])
    kv_heads = [g for g in range(N_KV_HEADS) for _ in range(2)]
    k_cols = np.concatenate([o_k + g * HEAD_DIM + perm for g in kv_heads])
    ks_cols = np.concatenate([o_k + g * HEAD_DIM + perm_sw for g in kv_heads])
    v_cols = np.concatenate([o_v + g * HEAD_DIM + np.arange(HEAD_DIM) for g in kv_heads])
    cols = np.concatenate([q_cols, qs_cols, k_cols, ks_cols, v_cols,
                           np.arange(o_cv, o_cg), np.arange(o_cg, o_cg + CONV_WIDTH)])
    assert cols.shape[0] == IN_COLS
    row2 = lambda a: a.reshape(1, -1)
    w_router_t = p["w_router"][l].T
    wr_hi, wr_lo = _split_bf16(w_router_t)
    w_out = p["w_out"][l].astype(BF16)
    return {
        "attn_norm_g": row2(p["attn_norm_g"][l]),
        "w_in": w_in[:, cols].astype(BF16),
        "gq": row2(jnp.tile(p["q_norm_g"][l][perm], N_HEADS)),
        "gqs": row2(jnp.tile(p["q_norm_g"][l][perm_sw], N_HEADS)),
        "gk": row2(jnp.tile(p["k_norm_g"][l][perm], 2 * N_KV_HEADS)),
        "gks": row2(jnp.tile(p["k_norm_g"][l][perm_sw], 2 * N_KV_HEADS)),
        "conv_dw_w": p["conv_dw_w"][l],
        "conv_dw_b": row2(p["conv_dw_b"][l]),
        "conv_ln_g": row2(p["conv_ln_g"][l]),
        "conv_ln_b": row2(p["conv_ln_b"][l]),
        "conv_out_g": row2(p["conv_out_g"][l]),
        "attn_out_g": row2(p["attn_out_g"][l]),
        "w_out_a": w_out[:ATTN_WIDTH],
        "w_out_c": w_out[ATTN_WIDTH:],
        "ffn_norm_g": row2(p["ffn_norm_g"][l]),
        "wr_hi": wr_hi,
        "wr_lo": wr_lo,
        "layer": l,
        "w_gate": p["w_gate"],
        "w_up": p["w_up"],
        "w_down": p["w_down"],
    }


def _moe(x1, hf, aff, lw):
    n = x1.shape[0]
    capacity = CAPACITY_FACTOR * n // N_EXPERTS
    nc = n // LANES
    slot3, off3 = _select(aff.reshape(N_EXPERTS, nc, LANES), capacity)
    bounds = jnp.concatenate([off3[:, :, 0], jnp.full((N_EXPERTS, 1), capacity, jnp.int32)], axis=1)
    slot2 = slot3.reshape(N_EXPERTS, n)
    xg, gate = _gather(bounds[:, ::GATHER_CHUNK // LANES], slot2, aff, hf, capacity)
    out = _ffn(xg, gate, lw["w_gate"], lw["w_up"], lw["w_down"], lw["layer"], capacity)
    lo = bounds[:, :nc].T
    return _combine((lo // BF16_ROWS) * BF16_ROWS, x1, slot2.T, out)


def _layer(x2d, batch, seq, lw, tabs):
    q, k, v, u = _in_proj(x2d, seq, lw, tabs)
    attn = _attention(q.reshape(batch, seq, ATTN_WIDTH), k.reshape(batch, seq, KV_DUP_WIDTH),
                      v.reshape(batch, seq, KV_DUP_WIDTH))
    conv = _conv(u.reshape(batch, seq, CONV_WIDTH), lw)
    x1, hf, aff = _out_proj(x2d, attn.reshape(-1, ATTN_WIDTH), conv.reshape(-1, CONV_WIDTH), lw)
    return _moe(x1, hf, aff, lw)


def _trunk(x, layers):
    batch, seq, d = x.shape
    tabs = _rope_tables(seq)
    x2d = x.reshape(batch * seq, d)
    for lw in layers:
        x2d = _layer(x2d, batch, seq, lw, tabs)
    return x2d.reshape(batch, seq, d)


def kernel(x_prompt, x_sample, attn_norm_g, w_in, q_norm_g, k_norm_g, conv_dw_w, conv_dw_b, conv_ln_g, conv_ln_b, attn_out_g, conv_out_g, w_out, ffn_norm_g, w_router, w_gate, w_up, w_down):
    p = dict(attn_norm_g=attn_norm_g, w_in=w_in, q_norm_g=q_norm_g, k_norm_g=k_norm_g,
             conv_dw_w=conv_dw_w, conv_dw_b=conv_dw_b, conv_ln_g=conv_ln_g, conv_ln_b=conv_ln_b,
             attn_out_g=attn_out_g, conv_out_g=conv_out_g, w_out=w_out, ffn_norm_g=ffn_norm_g,
             w_router=w_router, w_gate=w_gate, w_up=w_up, w_down=w_down)
    layers = [_layer_weights(l, p) for l in range(w_in.shape[0])]
    return (_trunk(x_prompt, layers), _trunk(x_sample, layers))
```

```python
import functools

import numpy as np
import jax
import jax.numpy as jnp
from jax import lax
from jax.experimental import pallas as pl
from jax.experimental.pallas import tpu as pltpu

F32 = jnp.float32
BF16 = jnp.bfloat16

D_MODEL = 1024
HEAD_DIM = 64
N_HEADS = 8
N_KV_HEADS = 2
ATTN_WIDTH = N_HEADS * HEAD_DIM
KV_WIDTH = N_KV_HEADS * HEAD_DIM
KV_DUP_WIDTH = 2 * KV_WIDTH
CONV_WIDTH = D_MODEL - ATTN_WIDTH
CONV_K = 31
GRID_W = 64
ROPE_THETA = 10000.0
AXIS_DIM = HEAD_DIM // 2
N_EXPERTS = 16
EXPERT_FF = 2 * D_MODEL
CAPACITY_FACTOR = 2
EPS = 1e-6

LANES = 128
SUBLANES = 8
BF16_ROWS = 16
VMEM_LIMIT_BYTES = 56 * 1024 * 1024

TOKEN_TILE = 512
Q_TILE = 1024
KV_CHUNK = 512
CONV_TILE = 128
CONV_HALO = 16
GATHER_CHUNK = 256
GATHER_SUB = 8
GATHER_EXPERTS = 2
GATHER_WINDOW = GATHER_CHUNK + BF16_ROWS
FF_CHUNK = 512
FFN_ROWS = 512
COMBINE_TILE = LANES
COMBINE_WINDOW = COMBINE_TILE + BF16_ROWS
COMBINE_COLS = 256
COMBINE_STEP_TILES = 2
CAP_PAD = COMBINE_WINDOW


def _params(*semantics):
    return pltpu.CompilerParams(dimension_semantics=semantics, vmem_limit_bytes=VMEM_LIMIT_BYTES)


def _split_bf16(a):
    hi = a.astype(BF16)
    lo = (a - hi.astype(F32)).astype(BF16)
    return hi, lo


W_Q = (0, 512)
W_K = (512, 768)
W_V = (768, 1024)
W_CV = (1024, 1536)
W_CG = (1536, 2048)
IN_COLS = 2048


def _swap_head_halves(a):
    half = HEAD_DIM // 2
    blocks = []
    for i in range(a.shape[1] // LANES):
        blk = a[:, i * LANES:(i + 1) * LANES]
        lane = lax.broadcasted_iota(jnp.int32, blk.shape, 1)
        first = (lane & (HEAD_DIM - 1)) < half
        blocks.append(jnp.where(first, pltpu.roll(blk, LANES - half, 1), pltpu.roll(blk, half, 1)))
    return jnp.concatenate(blocks, axis=1)


def _in_proj_kernel(x_ref, g_ref, w_ref, bq_ref, bk_ref, gq_ref, gqs_ref, gk_ref, gks_ref,
                    cq_ref, sq_ref, ck_ref, sk_ref, q_out, k_out, v_out, u_out):
    x = x_ref[...]
    ms = jnp.mean(x * x, axis=-1, keepdims=True)
    h = (x * lax.rsqrt(ms + EPS) * g_ref[...]).astype(BF16)
    proj = jnp.dot(h, w_ref[...], preferred_element_type=F32)

    def head_rsqrt(a, ones_blockdiag):
        hi, lo = _split_bf16(a * a)
        ss = (jnp.dot(hi, ones_blockdiag, preferred_element_type=F32)
              + jnp.dot(lo, ones_blockdiag, preferred_element_type=F32))
        return lax.rsqrt(ss * (1.0 / HEAD_DIM) + EPS)

    q = proj[:, W_Q[0]:W_Q[1]]
    qs = _swap_head_halves(q)
    rq = head_rsqrt(q, bq_ref[...])
    q_out[...] = ((q * rq * gq_ref[...]) * cq_ref[...]
                  + (qs * rq * gqs_ref[...]) * sq_ref[...]).astype(BF16)

    k = proj[:, W_K[0]:W_K[1]]
    ks = _swap_head_halves(k)
    rk = head_rsqrt(k, bk_ref[...])
    k_out[...] = ((k * rk * gk_ref[...]) * ck_ref[...]
                  + (ks * rk * gks_ref[...]) * sk_ref[...]).astype(BF16)

    v_out[...] = proj[:, W_V[0]:W_V[1]].astype(BF16)
    cv = proj[:, W_CV[0]:W_CV[1]]
    cg = proj[:, W_CG[0]:W_CG[1]]
    u_out[...] = cv * jax.nn.sigmoid(cg)


def _in_proj(x2d, seq, lw, tabs):
    n = x2d.shape[0]
    tm = TOKEN_TILE
    seq_tiles = seq // tm
    row = lambda i: (i, 0)
    fixed = lambda i: (0, 0)
    pos = lambda i: (i % seq_tiles, 0)
    full = lambda a: pl.BlockSpec(a.shape, fixed)
    return pl.pallas_call(
        _in_proj_kernel,
        grid=(n // tm,),
        in_specs=[
            pl.BlockSpec((tm, D_MODEL), row),
            full(lw["attn_norm_g"]), full(lw["w_in"]), full(tabs["bq"]), full(tabs["bk"]),
            full(lw["gq"]), full(lw["gqs"]), full(lw["gk"]), full(lw["gks"]),
            pl.BlockSpec((tm, ATTN_WIDTH), pos), pl.BlockSpec((tm, ATTN_WIDTH), pos),
            pl.BlockSpec((tm, KV_DUP_WIDTH), pos), pl.BlockSpec((tm, KV_DUP_WIDTH), pos),
        ],
        out_specs=[
            pl.BlockSpec((tm, ATTN_WIDTH), row),
            pl.BlockSpec((tm, KV_DUP_WIDTH), row),
            pl.BlockSpec((tm, KV_DUP_WIDTH), row),
            pl.BlockSpec((tm, CONV_WIDTH), row),
        ],
        out_shape=[
            jax.ShapeDtypeStruct((n, ATTN_WIDTH), BF16),
            jax.ShapeDtypeStruct((n, KV_DUP_WIDTH), BF16),
            jax.ShapeDtypeStruct((n, KV_DUP_WIDTH), BF16),
            jax.ShapeDtypeStruct((n, CONV_WIDTH), F32),
        ],
        compiler_params=_params("arbitrary"),
        name="in_proj",
    )(x2d, lw["attn_norm_g"], lw["w_in"], tabs["bq"], tabs["bk"],
      lw["gq"], lw["gqs"], lw["gk"], lw["gks"],
      tabs["cq"], tabs["sq"], tabs["ck"], tabs["sk"])


def _attention_kernel(q_ref, k_ref, v_ref, o_ref):
    q = q_ref[0]
    tq = q.shape[0]
    lane = lax.broadcasted_iota(jnp.int32, q.shape, 1)
    first = lane < HEAD_DIM
    zero = jnp.zeros_like(q)
    q2 = jnp.concatenate([jnp.where(first, q, zero), jnp.where(first, zero, q)], axis=0)
    seq = k_ref.shape[1]
    vlane = lax.broadcasted_iota(jnp.int32, (KV_CHUNK, LANES), 1) < HEAD_DIM
    one = jnp.ones((KV_CHUNK, LANES), BF16)
    m = jnp.full((2 * tq, 1), -jnp.inf, F32)
    acc_a = jnp.zeros((tq, LANES), F32)
    acc_b = jnp.zeros((tq, LANES), F32)
    for c in range(seq // KV_CHUNK):
        keys = pl.ds(c * KV_CHUNK, KV_CHUNK)
        s = lax.dot_general(q2, k_ref[0, keys, :], (((1,), (1,)), ((), ())), preferred_element_type=F32)
        m_new = jnp.maximum(m, jnp.max(s, axis=-1, keepdims=True))
        alpha = jnp.exp2(m - m_new)
        p = jnp.exp2(s - m_new).astype(BF16)
        v = v_ref[0, keys, :]
        acc_a = alpha[:tq] * acc_a + jnp.dot(p[:tq], jnp.where(vlane, v, one), preferred_element_type=F32)
        acc_b = alpha[tq:] * acc_b + jnp.dot(p[tq:], jnp.where(vlane, one, v), preferred_element_type=F32)
        m = m_new
    o_a = acc_a / pltpu.roll(acc_a, HEAD_DIM, 1)
    o_b = acc_b / pltpu.roll(acc_b, HEAD_DIM, 1)
    o_ref[0] = jnp.where(first, o_a, o_b)


def _attention(q, k, v):
    b, s, _ = q.shape
    pairs = N_HEADS // 2
    pairs_per_kv = pairs // N_KV_HEADS
    return pl.pallas_call(
        _attention_kernel,
        grid=(b, pairs, s // Q_TILE),
        in_specs=[
            pl.BlockSpec((1, Q_TILE, LANES), lambda bi, j, i: (bi, i, j)),
            pl.BlockSpec((1, s, LANES), lambda bi, j, i: (bi, 0, j // pairs_per_kv)),
            pl.BlockSpec((1, s, LANES), lambda bi, j, i: (bi, 0, j // pairs_per_kv)),
        ],
        out_specs=pl.BlockSpec((1, Q_TILE, LANES), lambda bi, j, i: (bi, i, j)),
        out_shape=jax.ShapeDtypeStruct((b, s, ATTN_WIDTH), F32),
        compiler_params=_params("arbitrary", "arbitrary", "arbitrary"),
        name="attention",
    )(q, k, v)


def _conv_kernel(up_ref, uc_ref, un_ref, w_ref, b_ref, lng_ref, lnb_ref, og_ref, o_ref, sh_ref):
    i = pl.program_id(1)
    last = pl.num_programs(1) - 1
    prev = jnp.where(i > 0, up_ref[0], 0.0)
    nxt = jnp.where(i < last, un_ref[0], 0.0)
    win = jnp.concatenate([prev, uc_ref[0], nxt], axis=0)
    tr = uc_ref.shape[1]
    w = w_ref[...]
    lead = CONV_HALO - CONV_K // 2
    shifted_len = tr + 2 * CONV_HALO - SUBLANES
    for rho in range(SUBLANES):
        sh_ref[rho] = win[rho:rho + shifted_len]
    parts = []
    for cb in range(CONV_WIDTH // LANES):
        lanes = slice(cb * LANES, (cb + 1) * LANES)
        part = jnp.zeros((tr, LANES), F32) + b_ref[:, lanes]
        for k in range(CONV_K):
            rho = (k + lead) % SUBLANES
            a = (k + lead) - rho
            part = part + sh_ref[rho, a:a + tr, lanes] * w[k:k + 1, lanes]
        parts.append(part)
    acc = jnp.concatenate(parts, axis=1)
    mu = jnp.mean(acc, axis=-1, keepdims=True)
    xc = acc - mu
    y = xc * lax.rsqrt(jnp.mean(xc * xc, axis=-1, keepdims=True) + EPS) * lng_ref[...] + lnb_ref[...]
    y = y * jax.nn.sigmoid(y)
    z = y * lax.rsqrt(jnp.mean(y * y, axis=-1, keepdims=True) + EPS) * og_ref[...]
    o_ref[0] = z.astype(BF16)


def _conv(u, lw):
    b, s, c = u.shape
    tr = CONV_TILE
    halo_per_tile = tr // CONV_HALO
    n_halo = s // CONV_HALO
    fixed = lambda bi, i: (0, 0)
    full = lambda a: pl.BlockSpec(a.shape, fixed)
    return pl.pallas_call(
        _conv_kernel,
        grid=(b, s // tr),
        in_specs=[
            pl.BlockSpec((1, CONV_HALO, c), lambda bi, i: (bi, jnp.maximum(i * halo_per_tile - 1, 0), 0)),
            pl.BlockSpec((1, tr, c), lambda bi, i: (bi, i, 0)),
            pl.BlockSpec((1, CONV_HALO, c),
                         lambda bi, i: (bi, jnp.minimum((i + 1) * halo_per_tile, n_halo - 1), 0)),
            full(lw["conv_dw_w"]), full(lw["conv_dw_b"]), full(lw["conv_ln_g"]),
            full(lw["conv_ln_b"]), full(lw["conv_out_g"]),
        ],
        out_specs=pl.BlockSpec((1, tr, c), lambda bi, i: (bi, i, 0)),
        out_shape=jax.ShapeDtypeStruct((b, s, c), BF16),
        scratch_shapes=[pltpu.VMEM((SUBLANES, tr + 2 * CONV_HALO - SUBLANES, c), F32)],
        compiler_params=_params("arbitrary", "arbitrary"),
        name="conv",
    )(u, u, u, lw["conv_dw_w"], lw["conv_dw_b"], lw["conv_ln_g"], lw["conv_ln_b"], lw["conv_out_g"])


def _out_proj_kernel(x_ref, a_ref, c_ref, ag_ref, wa_ref, wc_ref, fg_ref, wrh_ref, wrl_ref,
                     x1_out, hf_out, aff_out):
    a = a_ref[...]
    an = (a * lax.rsqrt(jnp.mean(a * a, axis=-1, keepdims=True) + EPS) * ag_ref[...]).astype(BF16)
    x1 = (x_ref[...]
          + jnp.dot(an, wa_ref[...], preferred_element_type=F32)
          + jnp.dot(c_ref[...], wc_ref[...], preferred_element_type=F32))
    x1_out[...] = x1
    hf = x1 * lax.rsqrt(jnp.mean(x1 * x1, axis=-1, keepdims=True) + EPS) * fg_ref[...]
    hf_hi, hf_lo = _split_bf16(hf)
    hf_out[...] = hf_hi
    nt = (((1,), (1,)), ((), ()))
    logits = (lax.dot_general(wrh_ref[...], hf_hi, nt, preferred_element_type=F32)
              + lax.dot_general(wrh_ref[...], hf_lo, nt, preferred_element_type=F32)
              + lax.dot_general(wrl_ref[...], hf_hi, nt, preferred_element_type=F32))
    mx = jnp.max(logits, axis=0, keepdims=True)
    ex = jnp.exp(logits - mx)
    aff_out[...] = ex / jnp.sum(ex, axis=0, keepdims=True)


def _out_proj(x2d, attn2d, conv2d, lw):
    n = x2d.shape[0]
    tm = TOKEN_TILE
    row = lambda i: (i, 0)
    fixed = lambda i: (0, 0)
    full = lambda a: pl.BlockSpec(a.shape, fixed)
    return pl.pallas_call(
        _out_proj_kernel,
        grid=(n // tm,),
        in_specs=[
            pl.BlockSpec((tm, D_MODEL), row),
            pl.BlockSpec((tm, ATTN_WIDTH), row),
            pl.BlockSpec((tm, CONV_WIDTH), row),
            full(lw["attn_out_g"]), full(lw["w_out_a"]), full(lw["w_out_c"]), full(lw["ffn_norm_g"]),
            full(lw["wr_hi"]), full(lw["wr_lo"]),
        ],
        out_specs=[
            pl.BlockSpec((tm, D_MODEL), row),
            pl.BlockSpec((tm, D_MODEL), row),
            pl.BlockSpec((N_EXPERTS, tm), lambda i: (0, i)),
        ],
        out_shape=[
            jax.ShapeDtypeStruct((n, D_MODEL), F32),
            jax.ShapeDtypeStruct((n, D_MODEL), BF16),
            jax.ShapeDtypeStruct((N_EXPERTS, n), F32),
        ],
        compiler_params=_params("arbitrary"),
        name="out_proj",
    )(x2d, attn2d, conv2d, lw["attn_out_g"], lw["w_out_a"], lw["w_out_c"], lw["ffn_norm_g"],
      lw["wr_hi"], lw["wr_lo"])


def _select_kernel(aff_ref, slot_out, off_out, *, capacity):
    a = aff_ref[...]
    e, nc, _ = a.shape
    keys = lax.bitcast_convert_type(a, jnp.int32)

    def count(pred):
        c = jnp.sum(pred.astype(F32), axis=1, keepdims=True)
        return jnp.sum(c, axis=2, keepdims=True)

    cap = float(capacity)

    def bit_step(i, t):
        cand = t | jnp.left_shift(jnp.int32(1), 30 - i)
        return jnp.where(count(keys >= cand) >= cap, cand, t)

    thr = lax.fori_loop(0, 31, bit_step, jnp.zeros((e, 1, 1), jnp.int32))
    above = keys > thr
    tied = keys == thr
    need = cap - count(above)

    li = lax.broadcasted_iota(jnp.int32, (LANES, LANES), 0)
    lj = lax.broadcasted_iota(jnp.int32, (LANES, LANES), 1)
    before_lane = (li < lj).astype(BF16)
    ci = lax.broadcasted_iota(jnp.int32, (nc, nc), 0)
    cj = lax.broadcasted_iota(jnp.int32, (nc, nc), 1)
    before_chunk = jnp.broadcast_to((cj < ci).astype(BF16)[None], (e, nc, nc))

    def prefix(pred):
        mb = pred.astype(BF16)
        inchunk = jnp.dot(mb.reshape(e * nc, LANES), before_lane,
                          preferred_element_type=F32).reshape(e, nc, LANES)
        part = lax.dot_general(before_chunk, mb, (((2,), (1,)), ((0,), (0,))),
                               preferred_element_type=F32)
        off = jnp.sum(part, axis=2, keepdims=True)
        return inchunk + off, off

    tie_rank, _ = prefix(tied)
    mask = above | (tied & (tie_rank < need))
    rank, off = prefix(mask)
    slot_out[...] = jnp.where(mask, rank, -1.0).astype(jnp.int32)
    off_out[...] = jnp.broadcast_to(off, off_out.shape).astype(jnp.int32)


def _select(aff3, capacity):
    e, nc, _ = aff3.shape
    return pl.pallas_call(
        functools.partial(_select_kernel, capacity=capacity),
        out_shape=[
            jax.ShapeDtypeStruct((e, nc, LANES), jnp.int32),
            jax.ShapeDtypeStruct((e, nc, LANES), jnp.int32),
        ],
        compiler_params=pltpu.CompilerParams(vmem_limit_bytes=VMEM_LIMIT_BYTES),
        name="select",
    )(aff3)


def _gather_kernel(bnd_ref, slot_ref, aff_ref, h_ref, xg_ref, gate_ref):
    eb = pl.program_id(0)
    s = pl.program_id(1)

    @pl.when(s == 0)
    def _():
        xg_ref[...] = jnp.zeros_like(xg_ref)
        gate_ref[...] = jnp.zeros_like(gate_ref)

    sub = lax.broadcasted_iota(jnp.int32, (GATHER_WINDOW, GATHER_CHUNK), 0)
    for ci in range(GATHER_SUB):
        h = h_ref[ci * GATHER_CHUNK:(ci + 1) * GATHER_CHUNK, :]
        for ee in range(GATHER_EXPERTS):
            e = eb * GATHER_EXPERTS + ee
            start = (bnd_ref[e, s * GATHER_SUB + ci] // BF16_ROWS) * BF16_ROWS
            srow = slot_ref[ee, ci:ci + 1, :]
            match = srow - start == sub
            rows = jnp.dot(match.astype(BF16), h, preferred_element_type=F32).astype(BF16)
            dst = pl.ds(pl.multiple_of(start, BF16_ROWS), GATHER_WINDOW)
            xg_ref[ee, dst, :] = xg_ref[ee, dst, :] + rows
            gate = jnp.sum(jnp.where(match, aff_ref[ee, ci:ci + 1, :], 0.0), axis=1, keepdims=True)
            gate_ref[ee, dst, :] = gate_ref[ee, dst, :] + gate


def _gather(bounds, slot2, aff, hf, capacity):
    n = hf.shape[0]
    chunks = n // GATHER_CHUNK
    rows = GATHER_SUB * GATHER_CHUNK
    padded = capacity + GATHER_WINDOW
    return pl.pallas_call(
        _gather_kernel,
        grid_spec=pltpu.PrefetchScalarGridSpec(
            num_scalar_prefetch=1,
            grid=(N_EXPERTS // GATHER_EXPERTS, chunks // GATHER_SUB),
            in_specs=[
                pl.BlockSpec((GATHER_EXPERTS, GATHER_SUB, GATHER_CHUNK), lambda e, s, b_: (e, s, 0)),
                pl.BlockSpec((GATHER_EXPERTS, GATHER_SUB, GATHER_CHUNK), lambda e, s, b_: (e, s, 0)),
                pl.BlockSpec((rows, D_MODEL), lambda e, s, b_: (s, 0)),
            ],
            out_specs=[
                pl.BlockSpec((GATHER_EXPERTS, padded, D_MODEL), lambda e, s, b_: (e, 0, 0)),
                pl.BlockSpec((GATHER_EXPERTS, padded, 1), lambda e, s, b_: (e, 0, 0)),
            ],
        ),
        out_shape=[
            jax.ShapeDtypeStruct((N_EXPERTS, padded, D_MODEL), BF16),
            jax.ShapeDtypeStruct((N_EXPERTS, padded, 1), F32),
        ],
        compiler_params=_params("arbitrary", "arbitrary"),
        name="gather",
    )(bounds, slot2.reshape(N_EXPERTS, chunks, GATHER_CHUNK), aff.reshape(N_EXPERTS, chunks, GATHER_CHUNK), hf)


def _ffn_kernel(xg_ref, gate_ref, wg_ref, wu_ref, wd_ref, o_ref, acc_ref):
    f = pl.program_id(1)

    @pl.when(f == 0)
    def _():
        acc_ref[...] = jnp.zeros_like(acc_ref)

    wg = wg_ref[0, 0].astype(BF16)
    wu = wu_ref[0, 0].astype(BF16)
    wd = wd_ref[0, 0].astype(BF16)
    cap = acc_ref.shape[0]
    block = min(FFN_ROWS, cap)
    for r in range(cap // block):
        rows = pl.ds(r * block, block)
        x = xg_ref[0, rows, :]
        a = jnp.dot(x, wg, preferred_element_type=F32)
        b = jnp.dot(x, wu, preferred_element_type=F32)
        h = (a * jax.nn.sigmoid(a) * b).astype(BF16)
        acc_ref[rows, :] += jnp.dot(h, wd, preferred_element_type=F32)

    @pl.when(f == pl.num_programs(1) - 1)
    def _():
        o_ref[0, pl.ds(0, cap), :] = (acc_ref[...] * gate_ref[0, pl.ds(0, cap), :]).astype(BF16)
        o_ref[0, pl.ds(cap, CAP_PAD), :] = jnp.zeros((CAP_PAD, o_ref.shape[2]), BF16)


def _ffn(xg, gate, w_gate, w_up, w_down, layer, capacity):
    e, _, d = xg.shape
    return pl.pallas_call(
        _ffn_kernel,
        grid=(e, EXPERT_FF // FF_CHUNK),
        in_specs=[
            pl.BlockSpec((1, capacity, d), lambda ei, f: (ei, 0, 0)),
            pl.BlockSpec((1, capacity, 1), lambda ei, f: (ei, 0, 0)),
            pl.BlockSpec((1, 1, d, FF_CHUNK), lambda ei, f: (layer, ei, 0, f)),
            pl.BlockSpec((1, 1, d, FF_CHUNK), lambda ei, f: (layer, ei, 0, f)),
            pl.BlockSpec((1, 1, FF_CHUNK, d), lambda ei, f: (layer, ei, f, 0)),
        ],
        out_specs=pl.BlockSpec((1, capacity + CAP_PAD, d), lambda ei, f: (ei, 0, 0)),
        out_shape=jax.ShapeDtypeStruct((e, capacity + CAP_PAD, d), BF16),
        scratch_shapes=[pltpu.VMEM((capacity, d), F32)],
        compiler_params=_params("arbitrary", "arbitrary"),
        name="ffn",
    )(xg, gate, w_gate, w_up, w_down)


def _combine_kernel(lo_ref, x1_ref, slot_ref, *rest):
    n_win = COMBINE_STEP_TILES * N_EXPERTS
    w_refs, y_ref = rest[:n_win], rest[n_win]
    step = pl.program_id(0)
    lane = lax.broadcasted_iota(jnp.int32, (COMBINE_TILE, COMBINE_WINDOW), 1)
    for tt in range(COMBINE_STEP_TILES):
        t = step * COMBINE_STEP_TILES + tt
        rows = pl.ds(tt * COMBINE_TILE, COMBINE_TILE)
        slot = slot_ref[rows, :]
        onehots = []
        for e in range(N_EXPERTS):
            rel = slot[:, e:e + 1] - lo_ref[t, e]
            onehots.append((rel == lane).astype(BF16))
        for cb in range(D_MODEL // COMBINE_COLS):
            cols = pl.ds(cb * COMBINE_COLS, COMBINE_COLS)
            y = x1_ref[rows, cols]
            for e in range(N_EXPERTS):
                y = y + jnp.dot(onehots[e], w_refs[tt * N_EXPERTS + e][0, :, cols],
                                preferred_element_type=F32)
            y_ref[rows, cols] = y


def _combine(lo_aligned, x1, slot_t, out):
    n = x1.shape[0]
    step_rows = COMBINE_STEP_TILES * COMBINE_TILE
    tok = lambda s, lo: (s, 0)

    def window(tt, e):
        return pl.BlockSpec(
            (pl.Element(1), pl.Element(COMBINE_WINDOW), pl.Element(D_MODEL)),
            lambda s, lo: (e, pl.multiple_of(lo[s * COMBINE_STEP_TILES + tt, e], BF16_ROWS), 0))

    windows = [window(tt, e) for tt in range(COMBINE_STEP_TILES) for e in range(N_EXPERTS)]
    return pl.pallas_call(
        _combine_kernel,
        grid_spec=pltpu.PrefetchScalarGridSpec(
            num_scalar_prefetch=1,
            grid=(n // step_rows,),
            in_specs=[
                pl.BlockSpec((step_rows, D_MODEL), tok),
                pl.BlockSpec((step_rows, N_EXPERTS), tok),
            ] + windows,
            out_specs=pl.BlockSpec((step_rows, D_MODEL), tok),
        ),
        out_shape=jax.ShapeDtypeStruct((n, D_MODEL), F32),
        compiler_params=_params("arbitrary"),
        name="combine",
    )(lo_aligned, x1, slot_t, *([out] * len(windows)))


def _head_perms():
    ev = np.arange(0, HEAD_DIM, 2)
    od = np.arange(1, HEAD_DIM, 2)
    return np.concatenate([ev, od]), np.concatenate([od, ev])


def _rope_tables(seq):
    rows = seq // GRID_W
    row = jnp.repeat(jnp.arange(rows, dtype=F32), GRID_W)
    col = jnp.tile(jnp.arange(GRID_W, dtype=F32), rows)
    inv_freq = ROPE_THETA ** (-jnp.arange(0, AXIS_DIM, 2, dtype=F32) / AXIS_DIM)
    ang = jnp.concatenate([row[:, None] * inv_freq, col[:, None] * inv_freq], axis=-1)
    cos, sin = jnp.cos(ang), jnp.sin(ang)
    c_head = jnp.concatenate([cos, cos], axis=-1)
    s_head = jnp.concatenate([-sin, sin], axis=-1)
    scale = HEAD_DIM ** -0.5 * float(np.log2(np.e))
    ones_block = jnp.ones((HEAD_DIM, HEAD_DIM), F32)
    return {
        "cq": jnp.tile(c_head, (1, N_HEADS)) * scale,
        "sq": jnp.tile(s_head, (1, N_HEADS)) * scale,
        "ck": jnp.tile(c_head, (1, 2 * N_KV_HEADS)),
        "sk": jnp.tile(s_head, (1, 2 * N_KV_HEADS)),
        "bq": jnp.kron(jnp.eye(N_HEADS, dtype=F32), ones_block).astype(BF16),
        "bk": jnp.kron(jnp.eye(2 * N_KV_HEADS, dtype=F32), ones_block).astype(BF16),
    }


def _layer_weights(l, p):
    perm, perm_sw = _head_perms()
    w_in = p["w_in"][l]
    o_k = ATTN_WIDTH
    o_v = o_k + KV_WIDTH
    o_cv = o_v + KV_WIDTH
    o_cg = o_cv + CONV_WIDTH
    q_cols = np.concatenate([h * HEAD_DIM + perm for h in range(N_HEADS)])
    kv_heads = [g for g in range(N_KV_HEADS) for _ in range(2)]
    k_cols = np.concatenate([o_k + g * HEAD_DIM + perm for g in kv_heads])
    v_cols = np.concatenate([o_v + g * HEAD_DIM + np.arange(HEAD_DIM) for g in kv_heads])
    cols = np.concatenate([q_cols, k_cols, v_cols,
                           np.arange(o_cv, o_cg), np.arange(o_cg, o_cg + CONV_WIDTH)])
    assert cols.shape[0] == IN_COLS
    row2 = lambda a: a.reshape(1, -1)
    w_router_t = p["w_router"][l].T
    wr_hi, wr_lo = _split_bf16(w_router_t)
    w_out = p["w_out"][l].astype(BF16)
    return {
        "attn_norm_g": row2(p["attn_norm_g"][l]),
        "w_in": w_in[:, cols].astype(BF16),
        "gq": row2(jnp.tile(p["q_norm_g"][l][perm], N_HEADS)),
        "gqs": row2(jnp.tile(p["q_norm_g"][l][perm_sw], N_HEADS)),
        "gk": row2(jnp.tile(p["k_norm_g"][l][perm], 2 * N_KV_HEADS)),
        "gks": row2(jnp.tile(p["k_norm_g"][l][perm_sw], 2 * N_KV_HEADS)),
        "conv_dw_w": p["conv_dw_w"][l],
        "conv_dw_b": row2(p["conv_dw_b"][l]),
        "conv_ln_g": row2(p["conv_ln_g"][l]),
        "conv_ln_b": row2(p["conv_ln_b"][l]),
        "conv_out_g": row2(p["conv_out_g"][l]),
        "attn_out_g": row2(p["attn_out_g"][l]),
        "w_out_a": w_out[:ATTN_WIDTH],
        "w_out_c": w_out[ATTN_WIDTH:],
        "ffn_norm_g": row2(p["ffn_norm_g"][l]),
        "wr_hi": wr_hi,
        "wr_lo": wr_lo,
        "layer": l,
        "w_gate": p["w_gate"],
        "w_up": p["w_up"],
        "w_down": p["w_down"],
    }


def _moe(x1, hf, aff, lw):
    n = x1.shape[0]
    capacity = CAPACITY_FACTOR * n // N_EXPERTS
    nc = n // LANES
    slot3, off3 = _select(aff.reshape(N_EXPERTS, nc, LANES), capacity)
    bounds = jnp.concatenate([off3[:, :, 0], jnp.full((N_EXPERTS, 1), capacity, jnp.int32)], axis=1)
    slot2 = slot3.reshape(N_EXPERTS, n)
    xg, gate = _gather(bounds[:, ::GATHER_CHUNK // LANES], slot2, aff, hf, capacity)
    out = _ffn(xg, gate, lw["w_gate"], lw["w_up"], lw["w_down"], lw["layer"], capacity)
    lo = bounds[:, :nc].T
    return _combine((lo // BF16_ROWS) * BF16_ROWS, x1, slot2.T, out)


def _layer(x2d, batch, seq, lw, tabs):
    q, k, v, u = _in_proj(x2d, seq, lw, tabs)
    attn = _attention(q.reshape(batch, seq, ATTN_WIDTH), k.reshape(batch, seq, KV_DUP_WIDTH),
                      v.reshape(batch, seq, KV_DUP_WIDTH))
    conv = _conv(u.reshape(batch, seq, CONV_WIDTH), lw)
    x1, hf, aff = _out_proj(x2d, attn.reshape(-1, ATTN_WIDTH), conv.reshape(-1, CONV_WIDTH), lw)
    return _moe(x1, hf, aff, lw)


def _trunk(x, layers):
    batch, seq, d = x.shape
    tabs = _rope_tables(seq)
    x2d = x.reshape(batch * seq, d)
    for lw in layers:
        x2d = _layer(x2d, batch, seq, lw, tabs)
    return x2d.reshape(batch, seq, d)


def kernel(x_prompt, x_sample, attn_norm_g, w_in, q_norm_g, k_norm_g, conv_dw_w, conv_dw_b, conv_ln_g, conv_ln_b, attn_out_g, conv_out_g, w_out, ffn_norm_g, w_router, w_gate, w_up, w_down):
    p = dict(attn_norm_g=attn_norm_g, w_in=w_in, q_norm_g=q_norm_g, k_norm_g=k_norm_g,
             conv_dw_w=conv_dw_w, conv_dw_b=conv_dw_b, conv_ln_g=conv_ln_g, conv_ln_b=conv_ln_b,
             attn_out_g=attn_out_g, conv_out_g=conv_out_g, w_out=w_out, ffn_norm_g=ffn_norm_g,
             w_router=w_router, w_gate=w_gate, w_up=w_up, w_down=w_down)
    layers = [_layer_weights(l, p) for l in range(w_in.shape[0])]
    return (_trunk(x_prompt, layers), _trunk(x_sample, layers))
```

```python
import functools

import numpy as np
import jax
import jax.numpy as jnp
from jax import lax
from jax.experimental import pallas as pl
from jax.experimental.pallas import tpu as pltpu

F32 = jnp.float32
BF16 = jnp.bfloat16

D_MODEL = 1024
HEAD_DIM = 64
N_HEADS = 8
N_KV_HEADS = 2
ATTN_WIDTH = N_HEADS * HEAD_DIM
KV_WIDTH = N_KV_HEADS * HEAD_DIM
KV_DUP_WIDTH = 2 * KV_WIDTH
CONV_WIDTH = D_MODEL - ATTN_WIDTH
CONV_K = 31
GRID_W = 64
ROPE_THETA = 10000.0
AXIS_DIM = HEAD_DIM // 2
N_EXPERTS = 16
EXPERT_FF = 2 * D_MODEL
CAPACITY_FACTOR = 2
EPS = 1e-6

LANES = 128
SUBLANES = 8
BF16_ROWS = 16
VMEM_LIMIT_BYTES = 56 * 1024 * 1024

TOKEN_TILE = 1024
Q_TILE = 1024
KV_CHUNK = 512
CONV_TILE = 256
CONV_HALO = 16
GATHER_CHUNK = 256
GATHER_SUB = 8
GATHER_EXPERTS = 2
GATHER_WINDOW = GATHER_CHUNK + BF16_ROWS
FF_CHUNK = 512
FFN_ROWS = 512
COMBINE_TILE = LANES
COMBINE_WINDOW = COMBINE_TILE + BF16_ROWS
COMBINE_COLS = 256
COMBINE_STEP_TILES = 4
CAP_PAD = COMBINE_WINDOW


def _params(*semantics):
    return pltpu.CompilerParams(dimension_semantics=semantics, vmem_limit_bytes=VMEM_LIMIT_BYTES)


def _split_bf16(a):
    hi = a.astype(BF16)
    lo = (a - hi.astype(F32)).astype(BF16)
    return hi, lo


W_Q = (0, 512)
W_K = (512, 768)
W_V = (768, 1024)
W_CV = (1024, 1536)
W_CG = (1536, 2048)
IN_COLS = 2048


def _swap_head_halves(a):
    half = HEAD_DIM // 2
    blocks = []
    for i in range(a.shape[1] // LANES):
        blk = a[:, i * LANES:(i + 1) * LANES]
        lane = lax.broadcasted_iota(jnp.int32, blk.shape, 1)
        first = (lane & (HEAD_DIM - 1)) < half
        blocks.append(jnp.where(first, pltpu.roll(blk, LANES - half, 1), pltpu.roll(blk, half, 1)))
    return jnp.concatenate(blocks, axis=1)


def _in_proj_kernel(x_ref, g_ref, w_ref, bq_ref, bk_ref, gq_ref, gqs_ref, gk_ref, gks_ref,
                    cq_ref, sq_ref, ck_ref, sk_ref, q_out, k_out, v_out, u_out):
    x = x_ref[...]
    ms = jnp.mean(x * x, axis=-1, keepdims=True)
    h = (x * lax.rsqrt(ms + EPS) * g_ref[...]).astype(BF16)
    proj = jnp.dot(h, w_ref[...], preferred_element_type=F32)

    def head_rsqrt(a, ones_blockdiag):
        hi, lo = _split_bf16(a * a)
        ss = (jnp.dot(hi, ones_blockdiag, preferred_element_type=F32)
              + jnp.dot(lo, ones_blockdiag, preferred_element_type=F32))
        return lax.rsqrt(ss * (1.0 / HEAD_DIM) + EPS)

    q = proj[:, W_Q[0]:W_Q[1]]
    qs = _swap_head_halves(q)
    rq = head_rsqrt(q, bq_ref[...])
    q_out[...] = ((q * rq * gq_ref[...]) * cq_ref[...]
                  + (qs * rq * gqs_ref[...]) * sq_ref[...]).astype(BF16)

    k = proj[:, W_K[0]:W_K[1]]
    ks = _swap_head_halves(k)
    rk = head_rsqrt(k, bk_ref[...])
    k_out[...] = ((k * rk * gk_ref[...]) * ck_ref[...]
                  + (ks * rk * gks_ref[...]) * sk_ref[...]).astype(BF16)

    v_out[...] = proj[:, W_V[0]:W_V[1]].astype(BF16)
    cv = proj[:, W_CV[0]:W_CV[1]]
    cg = proj[:, W_CG[0]:W_CG[1]]
    u_out[...] = cv * jax.nn.sigmoid(cg)


def _in_proj(x2d, seq, lw, tabs):
    n = x2d.shape[0]
    tm = TOKEN_TILE
    seq_tiles = seq // tm
    row = lambda i: (i, 0)
    fixed = lambda i: (0, 0)
    pos = lambda i: (i % seq_tiles, 0)
    full = lambda a: pl.BlockSpec(a.shape, fixed)
    return pl.pallas_call(
        _in_proj_kernel,
        grid=(n // tm,),
        in_specs=[
            pl.BlockSpec((tm, D_MODEL), row),
            full(lw["attn_norm_g"]), full(lw["w_in"]), full(tabs["bq"]), full(tabs["bk"]),
            full(lw["gq"]), full(lw["gqs"]), full(lw["gk"]), full(lw["gks"]),
            pl.BlockSpec((tm, ATTN_WIDTH), pos), pl.BlockSpec((tm, ATTN_WIDTH), pos),
            pl.BlockSpec((tm, KV_DUP_WIDTH), pos), pl.BlockSpec((tm, KV_DUP_WIDTH), pos),
        ],
        out_specs=[
            pl.BlockSpec((tm, ATTN_WIDTH), row),
            pl.BlockSpec((tm, KV_DUP_WIDTH), row),
            pl.BlockSpec((tm, KV_DUP_WIDTH), row),
            pl.BlockSpec((tm, CONV_WIDTH), row),
        ],
        out_shape=[
            jax.ShapeDtypeStruct((n, ATTN_WIDTH), BF16),
            jax.ShapeDtypeStruct((n, KV_DUP_WIDTH), BF16),
            jax.ShapeDtypeStruct((n, KV_DUP_WIDTH), BF16),
            jax.ShapeDtypeStruct((n, CONV_WIDTH), F32),
        ],
        compiler_params=_params("arbitrary"),
        name="in_proj",
    )(x2d, lw["attn_norm_g"], lw["w_in"], tabs["bq"], tabs["bk"],
      lw["gq"], lw["gqs"], lw["gk"], lw["gks"],
      tabs["cq"], tabs["sq"], tabs["ck"], tabs["sk"])


def _attention_kernel(q_ref, k_ref, v_ref, o_ref):
    q = q_ref[0]
    tq = q.shape[0]
    lane = lax.broadcasted_iota(jnp.int32, q.shape, 1)
    first = lane < HEAD_DIM
    zero = jnp.zeros_like(q)
    q2 = jnp.concatenate([jnp.where(first, q, zero), jnp.where(first, zero, q)], axis=0)
    seq = k_ref.shape[1]
    vlane = lax.broadcasted_iota(jnp.int32, (KV_CHUNK, LANES), 1) < HEAD_DIM
    one = jnp.ones((KV_CHUNK, LANES), BF16)
    m = jnp.full((2 * tq, 1), -jnp.inf, F32)
    acc_a = jnp.zeros((tq, LANES), F32)
    acc_b = jnp.zeros((tq, LANES), F32)
    for c in range(seq // KV_CHUNK):
        keys = pl.ds(c * KV_CHUNK, KV_CHUNK)
        s = lax.dot_general(q2, k_ref[0, keys, :], (((1,), (1,)), ((), ())), preferred_element_type=F32)
        m_new = jnp.maximum(m, jnp.max(s, axis=-1, keepdims=True))
        alpha = jnp.exp2(m - m_new)
        p = jnp.exp2(s - m_new).astype(BF16)
        v = v_ref[0, keys, :]
        acc_a = alpha[:tq] * acc_a + jnp.dot(p[:tq], jnp.where(vlane, v, one), preferred_element_type=F32)
        acc_b = alpha[tq:] * acc_b + jnp.dot(p[tq:], jnp.where(vlane, one, v), preferred_element_type=F32)
        m = m_new
    o_a = acc_a / pltpu.roll(acc_a, HEAD_DIM, 1)
    o_b = acc_b / pltpu.roll(acc_b, HEAD_DIM, 1)
    o_ref[0] = jnp.where(first, o_a, o_b)


def _attention(q, k, v):
    b, s, _ = q.shape
    pairs = N_HEADS // 2
    pairs_per_kv = pairs // N_KV_HEADS
    return pl.pallas_call(
        _attention_kernel,
        grid=(b, pairs, s // Q_TILE),
        in_specs=[
            pl.BlockSpec((1, Q_TILE, LANES), lambda bi, j, i: (bi, i, j)),
            pl.BlockSpec((1, s, LANES), lambda bi, j, i: (bi, 0, j // pairs_per_kv)),
            pl.BlockSpec((1, s, LANES), lambda bi, j, i: (bi, 0, j // pairs_per_kv)),
        ],
        out_specs=pl.BlockSpec((1, Q_TILE, LANES), lambda bi, j, i: (bi, i, j)),
        out_shape=jax.ShapeDtypeStruct((b, s, ATTN_WIDTH), F32),
        compiler_params=_params("arbitrary", "arbitrary", "arbitrary"),
        name="attention",
    )(q, k, v)


def _conv_kernel(up_ref, uc_ref, un_ref, w_ref, b_ref, lng_ref, lnb_ref, og_ref, o_ref, sh_ref):
    i = pl.program_id(1)
    last = pl.num_programs(1) - 1
    prev = jnp.where(i > 0, up_ref[0], 0.0)
    nxt = jnp.where(i < last, un_ref[0], 0.0)
    win = jnp.concatenate([prev, uc_ref[0], nxt], axis=0)
    tr = uc_ref.shape[1]
    w = w_ref[...]
    lead = CONV_HALO - CONV_K // 2
    shifted_len = tr + 2 * CONV_HALO - SUBLANES
    for rho in range(SUBLANES):
        sh_ref[rho] = win[rho:rho + shifted_len]
    parts = []
    for cb in range(CONV_WIDTH // LANES):
        lanes = slice(cb * LANES, (cb + 1) * LANES)
        part = jnp.zeros((tr, LANES), F32) + b_ref[:, lanes]
        for k in range(CONV_K):
            rho = (k + lead) % SUBLANES
            a = (k + lead) - rho
            part = part + sh_ref[rho, a:a + tr, lanes] * w[k:k + 1, lanes]
        parts.append(part)
    acc = jnp.concatenate(parts, axis=1)
    mu = jnp.mean(acc, axis=-1, keepdims=True)
    xc = acc - mu
    y = xc * lax.rsqrt(jnp.mean(xc * xc, axis=-1, keepdims=True) + EPS) * lng_ref[...] + lnb_ref[...]
    y = y * jax.nn.sigmoid(y)
    z = y * lax.rsqrt(jnp.mean(y * y, axis=-1, keepdims=True) + EPS) * og_ref[...]
    o_ref[0] = z.astype(BF16)


def _conv(u, lw):
    b, s, c = u.shape
    tr = CONV_TILE
    halo_per_tile = tr // CONV_HALO
    n_halo = s // CONV_HALO
    fixed = lambda bi, i: (0, 0)
    full = lambda a: pl.BlockSpec(a.shape, fixed)
    return pl.pallas_call(
        _conv_kernel,
        grid=(b, s // tr),
        in_specs=[
            pl.BlockSpec((1, CONV_HALO, c), lambda bi, i: (bi, jnp.maximum(i * halo_per_tile - 1, 0), 0)),
            pl.BlockSpec((1, tr, c), lambda bi, i: (bi, i, 0)),
            pl.BlockSpec((1, CONV_HALO, c),
                         lambda bi, i: (bi, jnp.minimum((i + 1) * halo_per_tile, n_halo - 1), 0)),
            full(lw["conv_dw_w"]), full(lw["conv_dw_b"]), full(lw["conv_ln_g"]),
            full(lw["conv_ln_b"]), full(lw["conv_out_g"]),
        ],
        out_specs=pl.BlockSpec((1, tr, c), lambda bi, i: (bi, i, 0)),
        out_shape=jax.ShapeDtypeStruct((b, s, c), BF16),
        scratch_shapes=[pltpu.VMEM((SUBLANES, tr + 2 * CONV_HALO - SUBLANES, c), F32)],
        compiler_params=_params("arbitrary", "arbitrary"),
        name="conv",
    )(u, u, u, lw["conv_dw_w"], lw["conv_dw_b"], lw["conv_ln_g"], lw["conv_ln_b"], lw["conv_out_g"])


def _out_proj_kernel(x_ref, a_ref, c_ref, ag_ref, wa_ref, wc_ref, fg_ref, wrh_ref, wrl_ref,
                     x1_out, hf_out, aff_out):
    a = a_ref[...]
    an = (a * lax.rsqrt(jnp.mean(a * a, axis=-1, keepdims=True) + EPS) * ag_ref[...]).astype(BF16)
    x1 = (x_ref[...]
          + jnp.dot(an, wa_ref[...], preferred_element_type=F32)
          + jnp.dot(c_ref[...], wc_ref[...], preferred_element_type=F32))
    x1_out[...] = x1
    hf = x1 * lax.rsqrt(jnp.mean(x1 * x1, axis=-1, keepdims=True) + EPS) * fg_ref[...]
    hf_hi, hf_lo = _split_bf16(hf)
    hf_out[...] = hf_hi
    nt = (((1,), (1,)), ((), ()))
    logits = (lax.dot_general(wrh_ref[...], hf_hi, nt, preferred_element_type=F32)
              + lax.dot_general(wrh_ref[...], hf_lo, nt, preferred_element_type=F32)
              + lax.dot_general(wrl_ref[...], hf_hi, nt, preferred_element_type=F32))
    mx = jnp.max(logits, axis=0, keepdims=True)
    ex = jnp.exp(logits - mx)
    aff_out[...] = ex / jnp.sum(ex, axis=0, keepdims=True)


def _out_proj(x2d, attn2d, conv2d, lw):
    n = x2d.shape[0]
    tm = TOKEN_TILE
    row = lambda i: (i, 0)
    fixed = lambda i: (0, 0)
    full = lambda a: pl.BlockSpec(a.shape, fixed)
    return pl.pallas_call(
        _out_proj_kernel,
        grid=(n // tm,),
        in_specs=[
            pl.BlockSpec((tm, D_MODEL), row),
            pl.BlockSpec((tm, ATTN_WIDTH), row),
            pl.BlockSpec((tm, CONV_WIDTH), row),
            full(lw["attn_out_g"]), full(lw["w_out_a"]), full(lw["w_out_c"]), full(lw["ffn_norm_g"]),
            full(lw["wr_hi"]), full(lw["wr_lo"]),
        ],
        out_specs=[
            pl.BlockSpec((tm, D_MODEL), row),
            pl.BlockSpec((tm, D_MODEL), row),
            pl.BlockSpec((N_EXPERTS, tm), lambda i: (0, i)),
        ],
        out_shape=[
            jax.ShapeDtypeStruct((n, D_MODEL), F32),
            jax.ShapeDtypeStruct((n, D_MODEL), BF16),
            jax.ShapeDtypeStruct((N_EXPERTS, n), F32),
        ],
        compiler_params=_params("arbitrary"),
        name="out_proj",
    )(x2d, attn2d, conv2d, lw["attn_out_g"], lw["w_out_a"], lw["w_out_c"], lw["ffn_norm_g"],
      lw["wr_hi"], lw["wr_lo"])


def _select_kernel(aff_ref, slot_out, off_out, *, capacity):
    a = aff_ref[...]
    e, nc, _ = a.shape
    keys = lax.bitcast_convert_type(a, jnp.int32)

    def count(pred):
        c = jnp.sum(pred.astype(F32), axis=1, keepdims=True)
        return jnp.sum(c, axis=2, keepdims=True)

    cap = float(capacity)

    def bit_step(i, t):
        cand = t | jnp.left_shift(jnp.int32(1), 30 - i)
        return jnp.where(count(keys >= cand) >= cap, cand, t)

    thr = lax.fori_loop(0, 31, bit_step, jnp.zeros((e, 1, 1), jnp.int32))
    above = keys > thr
    tied = keys == thr
    need = cap - count(above)

    li = lax.broadcasted_iota(jnp.int32, (LANES, LANES), 0)
    lj = lax.broadcasted_iota(jnp.int32, (LANES, LANES), 1)
    before_lane = (li < lj).astype(BF16)
    ci = lax.broadcasted_iota(jnp.int32, (nc, nc), 0)
    cj = lax.broadcasted_iota(jnp.int32, (nc, nc), 1)
    before_chunk = jnp.broadcast_to((cj < ci).astype(BF16)[None], (e, nc, nc))

    def prefix(pred):
        mb = pred.astype(BF16)
        inchunk = jnp.dot(mb.reshape(e * nc, LANES), before_lane,
                          preferred_element_type=F32).reshape(e, nc, LANES)
        part = lax.dot_general(before_chunk, mb, (((2,), (1,)), ((0,), (0,))),
                               preferred_element_type=F32)
        off = jnp.sum(part, axis=2, keepdims=True)
        return inchunk + off, off

    tie_rank, _ = prefix(tied)
    mask = above | (tied & (tie_rank < need))
    rank, off = prefix(mask)
    slot_out[...] = jnp.where(mask, rank, -1.0).astype(jnp.int32)
    off_out[...] = jnp.broadcast_to(off, off_out.shape).astype(jnp.int32)


def _select(aff3, capacity):
    e, nc, _ = aff3.shape
    return pl.pallas_call(
        functools.partial(_select_kernel, capacity=capacity),
        out_shape=[
            jax.ShapeDtypeStruct((e, nc, LANES), jnp.int32),
            jax.ShapeDtypeStruct((e, nc, LANES), jnp.int32),
        ],
        compiler_params=pltpu.CompilerParams(vmem_limit_bytes=VMEM_LIMIT_BYTES),
        name="select",
    )(aff3)


def _gather_kernel(bnd_ref, slot_ref, aff_ref, h_ref, xg_ref, gate_ref):
    eb = pl.program_id(0)
    s = pl.program_id(1)

    @pl.when(s == 0)
    def _():
        xg_ref[...] = jnp.zeros_like(xg_ref)
        gate_ref[...] = jnp.zeros_like(gate_ref)

    sub = lax.broadcasted_iota(jnp.int32, (GATHER_WINDOW, GATHER_CHUNK), 0)
    for ci in range(GATHER_SUB):
        h = h_ref[ci * GATHER_CHUNK:(ci + 1) * GATHER_CHUNK, :]
        for ee in range(GATHER_EXPERTS):
            e = eb * GATHER_EXPERTS + ee
            start = (bnd_ref[e, s * GATHER_SUB + ci] // BF16_ROWS) * BF16_ROWS
            srow = slot_ref[ee, ci:ci + 1, :]
            match = srow - start == sub
            rows = jnp.dot(match.astype(BF16), h, preferred_element_type=F32).astype(BF16)
            dst = pl.ds(pl.multiple_of(start, BF16_ROWS), GATHER_WINDOW)
            xg_ref[ee, dst, :] = xg_ref[ee, dst, :] + rows
            gate = jnp.sum(jnp.where(match, aff_ref[ee, ci:ci + 1, :], 0.0), axis=1, keepdims=True)
            gate_ref[ee, dst, :] = gate_ref[ee, dst, :] + gate


def _gather(bounds, slot2, aff, hf, capacity):
    n = hf.shape[0]
    chunks = n // GATHER_CHUNK
    rows = GATHER_SUB * GATHER_CHUNK
    padded = capacity + GATHER_WINDOW
    return pl.pallas_call(
        _gather_kernel,
        grid_spec=pltpu.PrefetchScalarGridSpec(
            num_scalar_prefetch=1,
            grid=(N_EXPERTS // GATHER_EXPERTS, chunks // GATHER_SUB),
            in_specs=[
                pl.BlockSpec((GATHER_EXPERTS, GATHER_SUB, GATHER_CHUNK), lambda e, s, b_: (e, s, 0)),
                pl.BlockSpec((GATHER_EXPERTS, GATHER_SUB, GATHER_CHUNK), lambda e, s, b_: (e, s, 0)),
                pl.BlockSpec((rows, D_MODEL), lambda e, s, b_: (s, 0)),
            ],
            out_specs=[
                pl.BlockSpec((GATHER_EXPERTS, padded, D_MODEL), lambda e, s, b_: (e, 0, 0)),
                pl.BlockSpec((GATHER_EXPERTS, padded, 1), lambda e, s, b_: (e, 0, 0)),
            ],
        ),
        out_shape=[
            jax.ShapeDtypeStruct((N_EXPERTS, padded, D_MODEL), BF16),
            jax.ShapeDtypeStruct((N_EXPERTS, padded, 1), F32),
        ],
        compiler_params=_params("arbitrary", "arbitrary"),
        name="gather",
    )(bounds, slot2.reshape(N_EXPERTS, chunks, GATHER_CHUNK), aff.reshape(N_EXPERTS, chunks, GATHER_CHUNK), hf)


def _ffn_kernel(xg_ref, gate_ref, wg_ref, wu_ref, wd_ref, o_ref, acc_ref):
    f = pl.program_id(1)

    @pl.when(f == 0)
    def _():
        acc_ref[...] = jnp.zeros_like(acc_ref)

    wg = wg_ref[0, 0].astype(BF16)
    wu = wu_ref[0, 0].astype(BF16)
    wd = wd_ref[0, 0].astype(BF16)
    cap = acc_ref.shape[0]
    block = min(FFN_ROWS, cap)
    for r in range(cap // block):
        rows = pl.ds(r * block, block)
        x = xg_ref[0, rows, :]
        a = jnp.dot(x, wg, preferred_element_type=F32)
        b = jnp.dot(x, wu, preferred_element_type=F32)
        h = (a * jax.nn.sigmoid(a) * b).astype(BF16)
        acc_ref[rows, :] += jnp.dot(h, wd, preferred_element_type=F32)

    @pl.when(f == pl.num_programs(1) - 1)
    def _():
        o_ref[0, pl.ds(0, cap), :] = (acc_ref[...] * gate_ref[0, pl.ds(0, cap), :]).astype(BF16)
        o_ref[0, pl.ds(cap, CAP_PAD), :] = jnp.zeros((CAP_PAD, o_ref.shape[2]), BF16)


def _ffn(xg, gate, w_gate, w_up, w_down, layer, capacity):
    e, _, d = xg.shape
    return pl.pallas_call(
        _ffn_kernel,
        grid=(e, EXPERT_FF // FF_CHUNK),
        in_specs=[
            pl.BlockSpec((1, capacity, d), lambda ei, f: (ei, 0, 0)),
            pl.BlockSpec((1, capacity, 1), lambda ei, f: (ei, 0, 0)),
            pl.BlockSpec((1, 1, d, FF_CHUNK), lambda ei, f: (layer, ei, 0, f)),
            pl.BlockSpec((1, 1, d, FF_CHUNK), lambda ei, f: (layer, ei, 0, f)),
            pl.BlockSpec((1, 1, FF_CHUNK, d), lambda ei, f: (layer, ei, f, 0)),
        ],
        out_specs=pl.BlockSpec((1, capacity + CAP_PAD, d), lambda ei, f: (ei, 0, 0)),
        out_shape=jax.ShapeDtypeStruct((e, capacity + CAP_PAD, d), BF16),
        scratch_shapes=[pltpu.VMEM((capacity, d), F32)],
        compiler_params=_params("arbitrary", "arbitrary"),
        name="ffn",
    )(xg, gate, w_gate, w_up, w_down)


def _combine_kernel(lo_ref, x1_ref, slot_ref, *rest):
    n_win = COMBINE_STEP_TILES * N_EXPERTS
    w_refs, y_ref = rest[:n_win], rest[n_win]
    step = pl.program_id(0)
    lane = lax.broadcasted_iota(jnp.int32, (COMBINE_TILE, COMBINE_WINDOW), 1)
    for tt in range(COMBINE_STEP_TILES):
        t = step * COMBINE_STEP_TILES + tt
        rows = pl.ds(tt * COMBINE_TILE, COMBINE_TILE)
        slot = slot_ref[rows, :]
        onehots = []
        for e in range(N_EXPERTS):
            rel = slot[:, e:e + 1] - lo_ref[t, e]
            onehots.append((rel == lane).astype(BF16))
        for cb in range(D_MODEL // COMBINE_COLS):
            cols = pl.ds(cb * COMBINE_COLS, COMBINE_COLS)
            y = x1_ref[rows, cols]
            for e in range(N_EXPERTS):
                y = y + jnp.dot(onehots[e], w_refs[tt * N_EXPERTS + e][0, :, cols],
                                preferred_element_type=F32)
            y_ref[rows, cols] = y


def _combine(lo_aligned, x1, slot_t, out):
    n = x1.shape[0]
    step_rows = COMBINE_STEP_TILES * COMBINE_TILE
    tok = lambda s, lo: (s, 0)

    def window(tt, e):
        return pl.BlockSpec(
            (pl.Element(1), pl.Element(COMBINE_WINDOW), pl.Element(D_MODEL)),
            lambda s, lo: (e, pl.multiple_of(lo[s * COMBINE_STEP_TILES + tt, e], BF16_ROWS), 0))

    windows = [window(tt, e) for tt in range(COMBINE_STEP_TILES) for e in range(N_EXPERTS)]
    return pl.pallas_call(
        _combine_kernel,
        grid_spec=pltpu.PrefetchScalarGridSpec(
            num_scalar_prefetch=1,
            grid=(n // step_rows,),
            in_specs=[
                pl.BlockSpec((step_rows, D_MODEL), tok),
                pl.BlockSpec((step_rows, N_EXPERTS), tok),
            ] + windows,
            out_specs=pl.BlockSpec((step_rows, D_MODEL), tok),
        ),
        out_shape=jax.ShapeDtypeStruct((n, D_MODEL), F32),
        compiler_params=_params("arbitrary"),
        name="combine",
    )(lo_aligned, x1, slot_t, *([out] * len(windows)))


def _head_perms():
    ev = np.arange(0, HEAD_DIM, 2)
    od = np.arange(1, HEAD_DIM, 2)
    return np.concatenate([ev, od]), np.concatenate([od, ev])


def _rope_tables(seq):
    rows = seq // GRID_W
    row = jnp.repeat(jnp.arange(rows, dtype=F32), GRID_W)
    col = jnp.tile(jnp.arange(GRID_W, dtype=F32), rows)
    inv_freq = ROPE_THETA ** (-jnp.arange(0, AXIS_DIM, 2, dtype=F32) / AXIS_DIM)
    ang = jnp.concatenate([row[:, None] * inv_freq, col[:, None] * inv_freq], axis=-1)
    cos, sin = jnp.cos(ang), jnp.sin(ang)
    c_head = jnp.concatenate([cos, cos], axis=-1)
    s_head = jnp.concatenate([-sin, sin], axis=-1)
    scale = HEAD_DIM ** -0.5 * float(np.log2(np.e))
    ones_block = jnp.ones((HEAD_DIM, HEAD_DIM), F32)
    return {
        "cq": jnp.tile(c_head, (1, N_HEADS)) * scale,
        "sq": jnp.tile(s_head, (1, N_HEADS)) * scale,
        "ck": jnp.tile(c_head, (1, 2 * N_KV_HEADS)),
        "sk": jnp.tile(s_head, (1, 2 * N_KV_HEADS)),
        "bq": jnp.kron(jnp.eye(N_HEADS, dtype=F32), ones_block).astype(BF16),
        "bk": jnp.kron(jnp.eye(2 * N_KV_HEADS, dtype=F32), ones_block).astype(BF16),
    }


def _layer_weights(l, p):
    perm, perm_sw = _head_perms()
    w_in = p["w_in"][l]
    o_k = ATTN_WIDTH
    o_v = o_k + KV_WIDTH
    o_cv = o_v + KV_WIDTH
    o_cg = o_cv + CONV_WIDTH
    q_cols = np.concatenate([h * HEAD_DIM + perm for h in range(N_HEADS)])
    kv_heads = [g for g in range(N_KV_HEADS) for _ in range(2)]
    k_cols = np.concatenate([o_k + g * HEAD_DIM + perm for g in kv_heads])
    v_cols = np.concatenate([o_v + g * HEAD_DIM + np.arange(HEAD_DIM) for g in kv_heads])
    cols = np.concatenate([q_cols, k_cols, v_cols,
                           np.arange(o_cv, o_cg), np.arange(o_cg, o_cg + CONV_WIDTH)])
    assert cols.shape[0] == IN_COLS
    row2 = lambda a: a.reshape(1, -1)
    w_router_t = p["w_router"][l].T
    wr_hi, wr_lo = _split_bf16(w_router_t)
    w_out = p["w_out"][l].astype(BF16)
    return {
        "attn_norm_g": row2(p["attn_norm_g"][l]),
        "w_in": w_in[:, cols].astype(BF16),
        "gq": row2(jnp.tile(p["q_norm_g"][l][perm], N_HEADS)),
        "gqs": row2(jnp.tile(p["q_norm_g"][l][perm_sw], N_HEADS)),
        "gk": row2(jnp.tile(p["k_norm_g"][l][perm], 2 * N_KV_HEADS)),
        "gks": row2(jnp.tile(p["k_norm_g"][l][perm_sw], 2 * N_KV_HEADS)),
        "conv_dw_w": p["conv_dw_w"][l],
        "conv_dw_b": row2(p["conv_dw_b"][l]),
        "conv_ln_g": row2(p["conv_ln_g"][l]),
        "conv_ln_b": row2(p["conv_ln_b"][l]),
        "conv_out_g": row2(p["conv_out_g"][l]),
        "attn_out_g": row2(p["attn_out_g"][l]),
        "w_out_a": w_out[:ATTN_WIDTH],
        "w_out_c": w_out[ATTN_WIDTH:],
        "ffn_norm_g": row2(p["ffn_norm_g"][l]),
        "wr_hi": wr_hi,
        "wr_lo": wr_lo,
        "layer": l,
        "w_gate": p["w_gate"],
        "w_up": p["w_up"],
        "w_down": p["w_down"],
    }


def _moe(x1, hf, aff, lw):
    n = x1.shape[0]
    capacity = CAPACITY_FACTOR * n // N_EXPERTS
    nc = n // LANES
    slot3, off3 = _select(aff.reshape(N_EXPERTS, nc, LANES), capacity)
    bounds = jnp.concatenate([off3[:, :, 0], jnp.full((N_EXPERTS, 1), capacity, jnp.int32)], axis=1)
    slot2 = slot3.reshape(N_EXPERTS, n)
    xg, gate = _gather(bounds[:, ::GATHER_CHUNK // LANES], slot2, aff, hf, capacity)
    out = _ffn(xg, gate, lw["w_gate"], lw["w_up"], lw["w_down"], lw["layer"], capacity)
    lo = bounds[:, :nc].T
    return _combine((lo // BF16_ROWS) * BF16_ROWS, x1, slot2.T, out)


def _layer(x2d, batch, seq, lw, tabs):
    q, k, v, u = _in_proj(x2d, seq, lw, tabs)
    attn = _attention(q.reshape(batch, seq, ATTN_WIDTH), k.reshape(batch, seq, KV_DUP_WIDTH),
                      v.reshape(batch, seq, KV_DUP_WIDTH))
    conv = _conv(u.reshape(batch, seq, CONV_WIDTH), lw)
    x1, hf, aff = _out_proj(x2d, attn.reshape(-1, ATTN_WIDTH), conv.reshape(-1, CONV_WIDTH), lw)
    return _moe(x1, hf, aff, lw)


def _trunk(x, layers):
    batch, seq, d = x.shape
    tabs = _rope_tables(seq)
    x2d = x.reshape(batch * seq, d)
    for lw in layers:
        x2d = _layer(x2d, batch, seq, lw, tabs)
    return x2d.reshape(batch, seq, d)


def kernel(x_prompt, x_sample, attn_norm_g, w_in, q_norm_g, k_norm_g, conv_dw_w, conv_dw_b, conv_ln_g, conv_ln_b, attn_out_g, conv_out_g, w_out, ffn_norm_g, w_router, w_gate, w_up, w_down):
    p = dict(attn_norm_g=attn_norm_g, w_in=w_in, q_norm_g=q_norm_g, k_norm_g=k_norm_g,
             conv_dw_w=conv_dw_w, conv_dw_b=conv_dw_b, conv_ln_g=conv_ln_g, conv_ln_b=conv_ln_b,
             attn_out_g=attn_out_g, conv_out_g=conv_out_g, w_out=w_out, ffn_norm_g=ffn_norm_g,
             w_router=w_router, w_gate=w_gate, w_up=w_up, w_down=w_down)
    layers = [_layer_weights(l, p) for l in range(w_in.shape[0])]
    return (_trunk(x_prompt, layers), _trunk(x_sample, layers))
```

```python
import functools

import numpy as np
import jax
import jax.numpy as jnp
from jax import lax
from jax.experimental import pallas as pl
from jax.experimental.pallas import tpu as pltpu

F32 = jnp.float32
BF16 = jnp.bfloat16

D_MODEL = 1024
HEAD_DIM = 64
N_HEADS = 8
N_KV_HEADS = 2
ATTN_WIDTH = N_HEADS * HEAD_DIM
KV_WIDTH = N_KV_HEADS * HEAD_DIM
KV_DUP_WIDTH = 2 * KV_WIDTH
CONV_WIDTH = D_MODEL - ATTN_WIDTH
CONV_K = 31
GRID_W = 64
ROPE_THETA = 10000.0
AXIS_DIM = HEAD_DIM // 2
N_EXPERTS = 16
EXPERT_FF = 2 * D_MODEL
CAPACITY_FACTOR = 2
EPS = 1e-6

LANES = 128
SUBLANES = 8
BF16_ROWS = 16
VMEM_LIMIT_BYTES = 56 * 1024 * 1024

TOKEN_TILE = 1024
Q_TILE = 1024
KV_CHUNK = 512
CONV_TILE = 256
CONV_HALO = 16
GATHER_CHUNK = 256
GATHER_SUB = 8
GATHER_EXPERTS = 2
GATHER_WINDOW = GATHER_CHUNK + BF16_ROWS
FF_CHUNK = 1024
FFN_ROWS = 512
COMBINE_TILE = LANES
COMBINE_WINDOW = COMBINE_TILE + BF16_ROWS
COMBINE_COLS = 256
COMBINE_STEP_TILES = 4
CAP_PAD = COMBINE_WINDOW


def _params(*semantics):
    return pltpu.CompilerParams(dimension_semantics=semantics, vmem_limit_bytes=VMEM_LIMIT_BYTES)


def _split_bf16(a):
    hi = a.astype(BF16)
    lo = (a - hi.astype(F32)).astype(BF16)
    return hi, lo


W_Q = (0, 512)
W_K = (512, 768)
W_V = (768, 1024)
W_CV = (1024, 1536)
W_CG = (1536, 2048)
IN_COLS = 2048


def _swap_head_halves(a):
    half = HEAD_DIM // 2
    blocks = []
    for i in range(a.shape[1] // LANES):
        blk = a[:, i * LANES:(i + 1) * LANES]
        lane = lax.broadcasted_iota(jnp.int32, blk.shape, 1)
        first = (lane & (HEAD_DIM - 1)) < half
        blocks.append(jnp.where(first, pltpu.roll(blk, LANES - half, 1), pltpu.roll(blk, half, 1)))
    return jnp.concatenate(blocks, axis=1)


def _in_proj_kernel(x_ref, g_ref, w_ref, bq_ref, bk_ref, gq_ref, gqs_ref, gk_ref, gks_ref,
                    cq_ref, sq_ref, ck_ref, sk_ref, q_out, k_out, v_out, u_out):
    x = x_ref[...]
    ms = jnp.mean(x * x, axis=-1, keepdims=True)
    h = (x * lax.rsqrt(ms + EPS) * g_ref[...]).astype(BF16)
    proj = jnp.dot(h, w_ref[...], preferred_element_type=F32)

    def head_rsqrt(a, ones_blockdiag):
        hi, lo = _split_bf16(a * a)
        ss = (jnp.dot(hi, ones_blockdiag, preferred_element_type=F32)
              + jnp.dot(lo, ones_blockdiag, preferred_element_type=F32))
        return lax.rsqrt(ss * (1.0 / HEAD_DIM) + EPS)

    q = proj[:, W_Q[0]:W_Q[1]]
    qs = _swap_head_halves(q)
    rq = head_rsqrt(q, bq_ref[...])
    q_out[...] = ((q * rq * gq_ref[...]) * cq_ref[...]
                  + (qs * rq * gqs_ref[...]) * sq_ref[...]).astype(BF16)

    k = proj[:, W_K[0]:W_K[1]]
    ks = _swap_head_halves(k)
    rk = head_rsqrt(k, bk_ref[...])
    k_out[...] = ((k * rk * gk_ref[...]) * ck_ref[...]
                  + (ks * rk * gks_ref[...]) * sk_ref[...]).astype(BF16)

    v_out[...] = proj[:, W_V[0]:W_V[1]].astype(BF16)
    cv = proj[:, W_CV[0]:W_CV[1]]
    cg = proj[:, W_CG[0]:W_CG[1]]
    u_out[...] = cv * jax.nn.sigmoid(cg)


def _in_proj(x2d, seq, lw, tabs):
    n = x2d.shape[0]
    tm = TOKEN_TILE
    seq_tiles = seq // tm
    row = lambda i: (i, 0)
    fixed = lambda i: (0, 0)
    pos = lambda i: (i % seq_tiles, 0)
    full = lambda a: pl.BlockSpec(a.shape, fixed)
    return pl.pallas_call(
        _in_proj_kernel,
        grid=(n // tm,),
        in_specs=[
            pl.BlockSpec((tm, D_MODEL), row),
            full(lw["attn_norm_g"]), full(lw["w_in"]), full(tabs["bq"]), full(tabs["bk"]),
            full(lw["gq"]), full(lw["gqs"]), full(lw["gk"]), full(lw["gks"]),
            pl.BlockSpec((tm, ATTN_WIDTH), pos), pl.BlockSpec((tm, ATTN_WIDTH), pos),
            pl.BlockSpec((tm, KV_DUP_WIDTH), pos), pl.BlockSpec((tm, KV_DUP_WIDTH), pos),
        ],
        out_specs=[
            pl.BlockSpec((tm, ATTN_WIDTH), row),
            pl.BlockSpec((tm, KV_DUP_WIDTH), row),
            pl.BlockSpec((tm, KV_DUP_WIDTH), row),
            pl.BlockSpec((tm, CONV_WIDTH), row),
        ],
        out_shape=[
            jax.ShapeDtypeStruct((n, ATTN_WIDTH), BF16),
            jax.ShapeDtypeStruct((n, KV_DUP_WIDTH), BF16),
            jax.ShapeDtypeStruct((n, KV_DUP_WIDTH), BF16),
            jax.ShapeDtypeStruct((n, CONV_WIDTH), F32),
        ],
        compiler_params=_params("arbitrary"),
        name="in_proj",
    )(x2d, lw["attn_norm_g"], lw["w_in"], tabs["bq"], tabs["bk"],
      lw["gq"], lw["gqs"], lw["gk"], lw["gks"],
      tabs["cq"], tabs["sq"], tabs["ck"], tabs["sk"])


def _attention_kernel(q_ref, k_ref, v_ref, o_ref):
    q = q_ref[0]
    tq = q.shape[0]
    lane = lax.broadcasted_iota(jnp.int32, q.shape, 1)
    first = lane < HEAD_DIM
    zero = jnp.zeros_like(q)
    q2 = jnp.concatenate([jnp.where(first, q, zero), jnp.where(first, zero, q)], axis=0)
    seq = k_ref.shape[1]
    vlane = lax.broadcasted_iota(jnp.int32, (KV_CHUNK, LANES), 1) < HEAD_DIM
    one = jnp.ones((KV_CHUNK, LANES), BF16)
    m = jnp.full((2 * tq, 1), -jnp.inf, F32)
    acc_a = jnp.zeros((tq, LANES), F32)
    acc_b = jnp.zeros((tq, LANES), F32)
    for c in range(seq // KV_CHUNK):
        keys = pl.ds(c * KV_CHUNK, KV_CHUNK)
        s = lax.dot_general(q2, k_ref[0, keys, :], (((1,), (1,)), ((), ())), preferred_element_type=F32)
        m_new = jnp.maximum(m, jnp.max(s, axis=-1, keepdims=True))
        alpha = jnp.exp2(m - m_new)
        p = jnp.exp2(s - m_new).astype(BF16)
        v = v_ref[0, keys, :]
        acc_a = alpha[:tq] * acc_a + jnp.dot(p[:tq], jnp.where(vlane, v, one), preferred_element_type=F32)
        acc_b = alpha[tq:] * acc_b + jnp.dot(p[tq:], jnp.where(vlane, one, v), preferred_element_type=F32)
        m = m_new
    o_a = acc_a / pltpu.roll(acc_a, HEAD_DIM, 1)
    o_b = acc_b / pltpu.roll(acc_b, HEAD_DIM, 1)
    o_ref[0] = jnp.where(first, o_a, o_b)


def _attention(q, k, v):
    b, s, _ = q.shape
    pairs = N_HEADS // 2
    pairs_per_kv = pairs // N_KV_HEADS
    return pl.pallas_call(
        _attention_kernel,
        grid=(b, pairs, s // Q_TILE),
        in_specs=[
            pl.BlockSpec((1, Q_TILE, LANES), lambda bi, j, i: (bi, i, j)),
            pl.BlockSpec((1, s, LANES), lambda bi, j, i: (bi, 0, j // pairs_per_kv)),
            pl.BlockSpec((1, s, LANES), lambda bi, j, i: (bi, 0, j // pairs_per_kv)),
        ],
        out_specs=pl.BlockSpec((1, Q_TILE, LANES), lambda bi, j, i: (bi, i, j)),
        out_shape=jax.ShapeDtypeStruct((b, s, ATTN_WIDTH), F32),
        compiler_params=_params("arbitrary", "arbitrary", "arbitrary"),
        name="attention",
    )(q, k, v)


def _conv_kernel(up_ref, uc_ref, un_ref, w_ref, b_ref, lng_ref, lnb_ref, og_ref, o_ref, sh_ref):
    i = pl.program_id(1)
    last = pl.num_programs(1) - 1
    prev = jnp.where(i > 0, up_ref[0], 0.0)
    nxt = jnp.where(i < last, un_ref[0], 0.0)
    win = jnp.concatenate([prev, uc_ref[0], nxt], axis=0)
    tr = uc_ref.shape[1]
    w = w_ref[...]
    lead = CONV_HALO - CONV_K // 2
    shifted_len = tr + 2 * CONV_HALO - SUBLANES
    for rho in range(SUBLANES):
        sh_ref[rho] = win[rho:rho + shifted_len]
    parts = []
    for cb in range(CONV_WIDTH // LANES):
        lanes = slice(cb * LANES, (cb + 1) * LANES)
        part = jnp.zeros((tr, LANES), F32) + b_ref[:, lanes]
        for k in range(CONV_K):
            rho = (k + lead) % SUBLANES
            a = (k + lead) - rho
            part = part + sh_ref[rho, a:a + tr, lanes] * w[k:k + 1, lanes]
        parts.append(part)
    acc = jnp.concatenate(parts, axis=1)
    mu = jnp.mean(acc, axis=-1, keepdims=True)
    xc = acc - mu
    y = xc * lax.rsqrt(jnp.mean(xc * xc, axis=-1, keepdims=True) + EPS) * lng_ref[...] + lnb_ref[...]
    y = y * jax.nn.sigmoid(y)
    z = y * lax.rsqrt(jnp.mean(y * y, axis=-1, keepdims=True) + EPS) * og_ref[...]
    o_ref[0] = z.astype(BF16)


def _conv(u, lw):
    b, s, c = u.shape
    tr = CONV_TILE
    halo_per_tile = tr // CONV_HALO
    n_halo = s // CONV_HALO
    fixed = lambda bi, i: (0, 0)
    full = lambda a: pl.BlockSpec(a.shape, fixed)
    return pl.pallas_call(
        _conv_kernel,
        grid=(b, s // tr),
        in_specs=[
            pl.BlockSpec((1, CONV_HALO, c), lambda bi, i: (bi, jnp.maximum(i * halo_per_tile - 1, 0), 0)),
            pl.BlockSpec((1, tr, c), lambda bi, i: (bi, i, 0)),
            pl.BlockSpec((1, CONV_HALO, c),
                         lambda bi, i: (bi, jnp.minimum((i + 1) * halo_per_tile, n_halo - 1), 0)),
            full(lw["conv_dw_w"]), full(lw["conv_dw_b"]), full(lw["conv_ln_g"]),
            full(lw["conv_ln_b"]), full(lw["conv_out_g"]),
        ],
        out_specs=pl.BlockSpec((1, tr, c), lambda bi, i: (bi, i, 0)),
        out_shape=jax.ShapeDtypeStruct((b, s, c), BF16),
        scratch_shapes=[pltpu.VMEM((SUBLANES, tr + 2 * CONV_HALO - SUBLANES, c), F32)],
        compiler_params=_params("arbitrary", "arbitrary"),
        name="conv",
    )(u, u, u, lw["conv_dw_w"], lw["conv_dw_b"], lw["conv_ln_g"], lw["conv_ln_b"], lw["conv_out_g"])


def _out_proj_kernel(x_ref, a_ref, c_ref, ag_ref, wa_ref, wc_ref, fg_ref, wrh_ref, wrl_ref,
                     x1_out, hf_out, aff_out):
    a = a_ref[...]
    an = (a * lax.rsqrt(jnp.mean(a * a, axis=-1, keepdims=True) + EPS) * ag_ref[...]).astype(BF16)
    x1 = (x_ref[...]
          + jnp.dot(an, wa_ref[...], preferred_element_type=F32)
          + jnp.dot(c_ref[...], wc_ref[...], preferred_element_type=F32))
    x1_out[...] = x1
    hf = x1 * lax.rsqrt(jnp.mean(x1 * x1, axis=-1, keepdims=True) + EPS) * fg_ref[...]
    hf_hi, hf_lo = _split_bf16(hf)
    hf_out[...] = hf_hi
    nt = (((1,), (1,)), ((), ()))
    logits = (lax.dot_general(wrh_ref[...], hf_hi, nt, preferred_element_type=F32)
              + lax.dot_general(wrh_ref[...], hf_lo, nt, preferred_element_type=F32)
              + lax.dot_general(wrl_ref[...], hf_hi, nt, preferred_element_type=F32))
    mx = jnp.max(logits, axis=0, keepdims=True)
    ex = jnp.exp(logits - mx)
    aff_out[...] = ex / jnp.sum(ex, axis=0, keepdims=True)


def _out_proj(x2d, attn2d, conv2d, lw):
    n = x2d.shape[0]
    tm = TOKEN_TILE
    row = lambda i: (i, 0)
    fixed = lambda i: (0, 0)
    full = lambda a: pl.BlockSpec(a.shape, fixed)
    return pl.pallas_call(
        _out_proj_kernel,
        grid=(n // tm,),
        in_specs=[
            pl.BlockSpec((tm, D_MODEL), row),
            pl.BlockSpec((tm, ATTN_WIDTH), row),
            pl.BlockSpec((tm, CONV_WIDTH), row),
            full(lw["attn_out_g"]), full(lw["w_out_a"]), full(lw["w_out_c"]), full(lw["ffn_norm_g"]),
            full(lw["wr_hi"]), full(lw["wr_lo"]),
        ],
        out_specs=[
            pl.BlockSpec((tm, D_MODEL), row),
            pl.BlockSpec((tm, D_MODEL), row),
            pl.BlockSpec((N_EXPERTS, tm), lambda i: (0, i)),
        ],
        out_shape=[
            jax.ShapeDtypeStruct((n, D_MODEL), F32),
            jax.ShapeDtypeStruct((n, D_MODEL), BF16),
            jax.ShapeDtypeStruct((N_EXPERTS, n), F32),
        ],
        compiler_params=_params("arbitrary"),
        name="out_proj",
    )(x2d, attn2d, conv2d, lw["attn_out_g"], lw["w_out_a"], lw["w_out_c"], lw["ffn_norm_g"],
      lw["wr_hi"], lw["wr_lo"])


def _select_kernel(aff_ref, slot_out, off_out, *, capacity):
    a = aff_ref[...]
    e, nc, _ = a.shape
    keys = lax.bitcast_convert_type(a, jnp.int32)

    def count(pred):
        c = jnp.sum(pred.astype(F32), axis=1, keepdims=True)
        return jnp.sum(c, axis=2, keepdims=True)

    cap = float(capacity)

    def bit_step(i, t):
        cand = t | jnp.left_shift(jnp.int32(1), 30 - i)
        return jnp.where(count(keys >= cand) >= cap, cand, t)

    thr = lax.fori_loop(0, 31, bit_step, jnp.zeros((e, 1, 1), jnp.int32))
    above = keys > thr
    tied = keys == thr
    need = cap - count(above)

    li = lax.broadcasted_iota(jnp.int32, (LANES, LANES), 0)
    lj = lax.broadcasted_iota(jnp.int32, (LANES, LANES), 1)
    before_lane = (li < lj).astype(BF16)
    ci = lax.broadcasted_iota(jnp.int32, (nc, nc), 0)
    cj = lax.broadcasted_iota(jnp.int32, (nc, nc), 1)
    before_chunk = jnp.broadcast_to((cj < ci).astype(BF16)[None], (e, nc, nc))

    def prefix(pred):
        mb = pred.astype(BF16)
        inchunk = jnp.dot(mb.reshape(e * nc, LANES), before_lane,
                          preferred_element_type=F32).reshape(e, nc, LANES)
        part = lax.dot_general(before_chunk, mb, (((2,), (1,)), ((0,), (0,))),
                               preferred_element_type=F32)
        off = jnp.sum(part, axis=2, keepdims=True)
        return inchunk + off, off

    tie_rank, _ = prefix(tied)
    mask = above | (tied & (tie_rank < need))
    rank, off = prefix(mask)
    slot_out[...] = jnp.where(mask, rank, -1.0).astype(jnp.int32)
    off_out[...] = jnp.broadcast_to(off, off_out.shape).astype(jnp.int32)


def _select(aff3, capacity):
    e, nc, _ = aff3.shape
    return pl.pallas_call(
        functools.partial(_select_kernel, capacity=capacity),
        out_shape=[
            jax.ShapeDtypeStruct((e, nc, LANES), jnp.int32),
            jax.ShapeDtypeStruct((e, nc, LANES), jnp.int32),
        ],
        compiler_params=pltpu.CompilerParams(vmem_limit_bytes=VMEM_LIMIT_BYTES),
        name="select",
    )(aff3)


def _gather_kernel(bnd_ref, slot_ref, aff_ref, h_ref, xg_ref, gate_ref):
    eb = pl.program_id(0)
    s = pl.program_id(1)

    @pl.when(s == 0)
    def _():
        xg_ref[...] = jnp.zeros_like(xg_ref)
        gate_ref[...] = jnp.zeros_like(gate_ref)

    sub = lax.broadcasted_iota(jnp.int32, (GATHER_WINDOW, GATHER_CHUNK), 0)
    for ci in range(GATHER_SUB):
        h = h_ref[ci * GATHER_CHUNK:(ci + 1) * GATHER_CHUNK, :]
        for ee in range(GATHER_EXPERTS):
            e = eb * GATHER_EXPERTS + ee
            start = (bnd_ref[e, s * GATHER_SUB + ci] // BF16_ROWS) * BF16_ROWS
            srow = slot_ref[ee, ci:ci + 1, :]
            match = srow - start == sub
            rows = jnp.dot(match.astype(BF16), h, preferred_element_type=F32).astype(BF16)
            dst = pl.ds(pl.multiple_of(start, BF16_ROWS), GATHER_WINDOW)
            xg_ref[ee, dst, :] = xg_ref[ee, dst, :] + rows
            gate = jnp.sum(jnp.where(match, aff_ref[ee, ci:ci + 1, :], 0.0), axis=1, keepdims=True)
            gate_ref[ee, dst, :] = gate_ref[ee, dst, :] + gate


def _gather(bounds, slot2, aff, hf, capacity):
    n = hf.shape[0]
    chunks = n // GATHER_CHUNK
    rows = GATHER_SUB * GATHER_CHUNK
    padded = capacity + GATHER_WINDOW
    return pl.pallas_call(
        _gather_kernel,
        grid_spec=pltpu.PrefetchScalarGridSpec(
            num_scalar_prefetch=1,
            grid=(N_EXPERTS // GATHER_EXPERTS, chunks // GATHER_SUB),
            in_specs=[
                pl.BlockSpec((GATHER_EXPERTS, GATHER_SUB, GATHER_CHUNK), lambda e, s, b_: (e, s, 0)),
                pl.BlockSpec((GATHER_EXPERTS, GATHER_SUB, GATHER_CHUNK), lambda e, s, b_: (e, s, 0)),
                pl.BlockSpec((rows, D_MODEL), lambda e, s, b_: (s, 0)),
            ],
            out_specs=[
                pl.BlockSpec((GATHER_EXPERTS, padded, D_MODEL), lambda e, s, b_: (e, 0, 0)),
                pl.BlockSpec((GATHER_EXPERTS, padded, 1), lambda e, s, b_: (e, 0, 0)),
            ],
        ),
        out_shape=[
            jax.ShapeDtypeStruct((N_EXPERTS, padded, D_MODEL), BF16),
            jax.ShapeDtypeStruct((N_EXPERTS, padded, 1), F32),
        ],
        compiler_params=_params("arbitrary", "arbitrary"),
        name="gather",
    )(bounds, slot2.reshape(N_EXPERTS, chunks, GATHER_CHUNK), aff.reshape(N_EXPERTS, chunks, GATHER_CHUNK), hf)


def _ffn_kernel(xg_ref, gate_ref, wg_ref, wu_ref, wd_ref, o_ref, acc_ref):
    f = pl.program_id(1)

    @pl.when(f == 0)
    def _():
        acc_ref[...] = jnp.zeros_like(acc_ref)

    wg = wg_ref[0, 0].astype(BF16)
    wu = wu_ref[0, 0].astype(BF16)
    wd = wd_ref[0, 0].astype(BF16)
    cap = acc_ref.shape[0]
    block = min(FFN_ROWS, cap)
    for r in range(cap // block):
        rows = pl.ds(r * block, block)
        x = xg_ref[0, rows, :]
        a = jnp.dot(x, wg, preferred_element_type=F32)
        b = jnp.dot(x, wu, preferred_element_type=F32)
        h = (a * jax.nn.sigmoid(a) * b).astype(BF16)
        acc_ref[rows, :] += jnp.dot(h, wd, preferred_element_type=F32)

    @pl.when(f == pl.num_programs(1) - 1)
    def _():
        o_ref[0, pl.ds(0, cap), :] = (acc_ref[...] * gate_ref[0, pl.ds(0, cap), :]).astype(BF16)
        o_ref[0, pl.ds(cap, CAP_PAD), :] = jnp.zeros((CAP_PAD, o_ref.shape[2]), BF16)


def _ffn(xg, gate, w_gate, w_up, w_down, layer, capacity):
    e, _, d = xg.shape
    return pl.pallas_call(
        _ffn_kernel,
        grid=(e, EXPERT_FF // FF_CHUNK),
        in_specs=[
            pl.BlockSpec((1, capacity, d), lambda ei, f: (ei, 0, 0), pipeline_mode=pl.Buffered(1)),
            pl.BlockSpec((1, capacity, 1), lambda ei, f: (ei, 0, 0), pipeline_mode=pl.Buffered(1)),
            pl.BlockSpec((1, 1, d, FF_CHUNK), lambda ei, f: (layer, ei, 0, f)),
            pl.BlockSpec((1, 1, d, FF_CHUNK), lambda ei, f: (layer, ei, 0, f)),
            pl.BlockSpec((1, 1, FF_CHUNK, d), lambda ei, f: (layer, ei, f, 0)),
        ],
        out_specs=pl.BlockSpec((1, capacity + CAP_PAD, d), lambda ei, f: (ei, 0, 0)),
        out_shape=jax.ShapeDtypeStruct((e, capacity + CAP_PAD, d), BF16),
        scratch_shapes=[pltpu.VMEM((capacity, d), F32)],
        compiler_params=_params("arbitrary", "arbitrary"),
        name="ffn",
    )(xg, gate, w_gate, w_up, w_down)


def _combine_kernel(lo_ref, x1_ref, slot_ref, *rest):
    n_win = COMBINE_STEP_TILES * N_EXPERTS
    w_refs, y_ref = rest[:n_win], rest[n_win]
    step = pl.program_id(0)
    lane = lax.broadcasted_iota(jnp.int32, (COMBINE_TILE, COMBINE_WINDOW), 1)
    for tt in range(COMBINE_STEP_TILES):
        t = step * COMBINE_STEP_TILES + tt
        rows = pl.ds(tt * COMBINE_TILE, COMBINE_TILE)
        slot = slot_ref[rows, :]
        onehots = []
        for e in range(N_EXPERTS):
            rel = slot[:, e:e + 1] - lo_ref[t, e]
            onehots.append((rel == lane).astype(BF16))
        for cb in range(D_MODEL // COMBINE_COLS):
            cols = pl.ds(cb * COMBINE_COLS, COMBINE_COLS)
            y = x1_ref[rows, cols]
            for e in range(N_EXPERTS):
                y = y + jnp.dot(onehots[e], w_refs[tt * N_EXPERTS + e][0, :, cols],
                                preferred_element_type=F32)
            y_ref[rows, cols] = y


def _combine(lo_aligned, x1, slot_t, out):
    n = x1.shape[0]
    step_rows = COMBINE_STEP_TILES * COMBINE_TILE
    tok = lambda s, lo: (s, 0)

    def window(tt, e):
        return pl.BlockSpec(
            (pl.Element(1), pl.Element(COMBINE_WINDOW), pl.Element(D_MODEL)),
            lambda s, lo: (e, pl.multiple_of(lo[s * COMBINE_STEP_TILES + tt, e], BF16_ROWS), 0))

    windows = [window(tt, e) for tt in range(COMBINE_STEP_TILES) for e in range(N_EXPERTS)]
    return pl.pallas_call(
        _combine_kernel,
        grid_spec=pltpu.PrefetchScalarGridSpec(
            num_scalar_prefetch=1,
            grid=(n // step_rows,),
            in_specs=[
                pl.BlockSpec((step_rows, D_MODEL), tok),
                pl.BlockSpec((step_rows, N_EXPERTS), tok),
            ] + windows,
            out_specs=pl.BlockSpec((step_rows, D_MODEL), tok),
        ),
        out_shape=jax.ShapeDtypeStruct((n, D_MODEL), F32),
        compiler_params=_params("arbitrary"),
        name="combine",
    )(lo_aligned, x1, slot_t, *([out] * len(windows)))


def _head_perms():
    ev = np.arange(0, HEAD_DIM, 2)
    od = np.arange(1, HEAD_DIM, 2)
    return np.concatenate([ev, od]), np.concatenate([od, ev])


def _rope_tables(seq):
    rows = seq // GRID_W
    row = jnp.repeat(jnp.arange(rows, dtype=F32), GRID_W)
    col = jnp.tile(jnp.arange(GRID_W, dtype=F32), rows)
    inv_freq = ROPE_THETA ** (-jnp.arange(0, AXIS_DIM, 2, dtype=F32) / AXIS_DIM)
    ang = jnp.concatenate([row[:, None] * inv_freq, col[:, None] * inv_freq], axis=-1)
    cos, sin = jnp.cos(ang), jnp.sin(ang)
    c_head = jnp.concatenate([cos, cos], axis=-1)
    s_head = jnp.concatenate([-sin, sin], axis=-1)
    scale = HEAD_DIM ** -0.5 * float(np.log2(np.e))
    ones_block = jnp.ones((HEAD_DIM, HEAD_DIM), F32)
    return {
        "cq": jnp.tile(c_head, (1, N_HEADS)) * scale,
        "sq": jnp.tile(s_head, (1, N_HEADS)) * scale,
        "ck": jnp.tile(c_head, (1, 2 * N_KV_HEADS)),
        "sk": jnp.tile(s_head, (1, 2 * N_KV_HEADS)),
        "bq": jnp.kron(jnp.eye(N_HEADS, dtype=F32), ones_block).astype(BF16),
        "bk": jnp.kron(jnp.eye(2 * N_KV_HEADS, dtype=F32), ones_block).astype(BF16),
    }


def _layer_weights(l, p):
    perm, perm_sw = _head_perms()
    w_in = p["w_in"][l]
    o_k = ATTN_WIDTH
    o_v = o_k + KV_WIDTH
    o_cv = o_v + KV_WIDTH
    o_cg = o_cv + CONV_WIDTH
    q_cols = np.concatenate([h * HEAD_DIM + perm for h in range(N_HEADS)])
    kv_heads = [g for g in range(N_KV_HEADS) for _ in range(2)]
    k_cols = np.concatenate([o_k + g * HEAD_DIM + perm for g in kv_heads])
    v_cols = np.concatenate([o_v + g * HEAD_DIM + np.arange(HEAD_DIM) for g in kv_heads])
    cols = np.concatenate([q_cols, k_cols, v_cols,
                           np.arange(o_cv, o_cg), np.arange(o_cg, o_cg + CONV_WIDTH)])
    assert cols.shape[0] == IN_COLS
    row2 = lambda a: a.reshape(1, -1)
    w_router_t = p["w_router"][l].T
    wr_hi, wr_lo = _split_bf16(w_router_t)
    w_out = p["w_out"][l].astype(BF16)
    return {
        "attn_norm_g": row2(p["attn_norm_g"][l]),
        "w_in": w_in[:, cols].astype(BF16),
        "gq": row2(jnp.tile(p["q_norm_g"][l][perm], N_HEADS)),
        "gqs": row2(jnp.tile(p["q_norm_g"][l][perm_sw], N_HEADS)),
        "gk": row2(jnp.tile(p["k_norm_g"][l][perm], 2 * N_KV_HEADS)),
        "gks": row2(jnp.tile(p["k_norm_g"][l][perm_sw], 2 * N_KV_HEADS)),
        "conv_dw_w": p["conv_dw_w"][l],
        "conv_dw_b": row2(p["conv_dw_b"][l]),
        "conv_ln_g": row2(p["conv_ln_g"][l]),
        "conv_ln_b": row2(p["conv_ln_b"][l]),
        "conv_out_g": row2(p["conv_out_g"][l]),
        "attn_out_g": row2(p["attn_out_g"][l]),
        "w_out_a": w_out[:ATTN_WIDTH],
        "w_out_c": w_out[ATTN_WIDTH:],
        "ffn_norm_g": row2(p["ffn_norm_g"][l]),
        "wr_hi": wr_hi,
        "wr_lo": wr_lo,
        "layer": l,
        "w_gate": p["w_gate"],
        "w_up": p["w_up"],
        "w_down": p["w_down"],
    }


def _moe(x1, hf, aff, lw):
    n = x1.shape[0]
    capacity = CAPACITY_FACTOR * n // N_EXPERTS
    nc = n // LANES
    slot3, off3 = _select(aff.reshape(N_EXPERTS, nc, LANES), capacity)
    bounds = jnp.concatenate([off3[:, :, 0], jnp.full((N_EXPERTS, 1), capacity, jnp.int32)], axis=1)
    slot2 = slot3.reshape(N_EXPERTS, n)
    xg, gate = _gather(bounds[:, ::GATHER_CHUNK // LANES], slot2, aff, hf, capacity)
    out = _ffn(xg, gate, lw["w_gate"], lw["w_up"], lw["w_down"], lw["layer"], capacity)
    lo = bounds[:, :nc].T
    return _combine((lo // BF16_ROWS) * BF16_ROWS, x1, slot2.T, out)


def _layer(x2d, batch, seq, lw, tabs):
    q, k, v, u = _in_proj(x2d, seq, lw, tabs)
    attn = _attention(q.reshape(batch, seq, ATTN_WIDTH), k.reshape(batch, seq, KV_DUP_WIDTH),
                      v.reshape(batch, seq, KV_DUP_WIDTH))
    conv = _conv(u.reshape(batch, seq, CONV_WIDTH), lw)
    x1, hf, aff = _out_proj(x2d, attn.reshape(-1, ATTN_WIDTH), conv.reshape(-1, CONV_WIDTH), lw)
    return _moe(x1, hf, aff, lw)


def _trunk(x, layers):
    batch, seq, d = x.shape
    tabs = _rope_tables(seq)
    x2d = x.reshape(batch * seq, d)
    for lw in layers:
        x2d = _layer(x2d, batch, seq, lw, tabs)
    return x2d.reshape(batch, seq, d)


def kernel(x_prompt, x_sample, attn_norm_g, w_in, q_norm_g, k_norm_g, conv_dw_w, conv_dw_b, conv_ln_g, conv_ln_b, attn_out_g, conv_out_g, w_out, ffn_norm_g, w_router, w_gate, w_up, w_down):
    p = dict(attn_norm_g=attn_norm_g, w_in=w_in, q_norm_g=q_norm_g, k_norm_g=k_norm_g,
             conv_dw_w=conv_dw_w, conv_dw_b=conv_dw_b, conv_ln_g=conv_ln_g, conv_ln_b=conv_ln_b,
             attn_out_g=attn_out_g, conv_out_g=conv_out_g, w_out=w_out, ffn_norm_g=ffn_norm_g,
             w_router=w_router, w_gate=w_gate, w_up=w_up, w_down=w_down)
    layers = [_layer_weights(l, p) for l in range(w_in.shape[0])]
    return (_trunk(x_prompt, layers), _trunk(x_sample, layers))
```
